```python
import math
import jax, jax.numpy as jnp
from jax import lax
import numpy as np

D_MODEL = 4096
BATCH = 1
SEQ = 8192
DEPTH = 2

ATTN_W = D_MODEL // 2
CONV_W = D_MODEL // 4
SGU_W = D_MODEL - ATTN_W - CONV_W
DIFF_HEAD_DIM = 64
ATTN_HEADS = ATTN_W // (2 * DIFF_HEAD_DIM)
CONV_K = 31
CHUNK = 128
SGU_GROUP_W = 128
SGU_GROUPS = SGU_W // SGU_GROUP_W
Q_BLOCK = 128
ROPE_THETA = 10000.0
LN_EPS = 1e-5
RMS_EPS = 1e-5
DEEPNORM_ALPHA = (2 * DEPTH) ** 0.25
DEEPNORM_BETA = (8 * DEPTH) ** -0.25
SPLITS = (ATTN_W, ATTN_W, ATTN_W, ATTN_W, CONV_W, CONV_W, CONV_W, SGU_W, SGU_W, SGU_W)
IN_W = sum(SPLITS)
SPLIT_IDX = tuple(int(i) for i in np.cumsum(SPLITS)[:-1])

kernel_name = "hymba_style_diffattn_conformer_gmlp_deepnorm"


def layer_norm(x, g, b):
    x32 = x.astype(jnp.float32)
    mu = jnp.mean(x32, axis=-1, keepdims=True)
    var = jnp.mean(jnp.square(x32 - mu), axis=-1, keepdims=True)
    y = (x32 - mu) * lax.rsqrt(var + LN_EPS) * g.astype(jnp.float32) + b.astype(jnp.float32)
    return y.astype(x.dtype)


def rope_tables(positions, dtype):
    inv_freq = ROPE_THETA ** (-jnp.arange(0, DIFF_HEAD_DIM, 2, dtype=jnp.float32) / DIFF_HEAD_DIM)
    ang = positions.astype(jnp.float32)[..., None] * inv_freq
    ang = jnp.concatenate([ang, ang], axis=-1)[:, :, None, None, :]
    return jnp.cos(ang).astype(dtype), jnp.sin(ang).astype(dtype)


def apply_rope(t, cos, sin):
    half = DIFF_HEAD_DIM // 2
    rot = jnp.concatenate([-t[..., half:], t[..., :half]], axis=-1)
    return t * cos + rot * sin


def diff_attention(q, k, v, lam):
    B, S = q.shape[0], q.shape[1]
    nb = S // Q_BLOCK
    qb = jnp.swapaxes(q.reshape(B, nb, Q_BLOCK, ATTN_HEADS, 2, DIFF_HEAD_DIM), 0, 1)
    kpos = jnp.arange(S)
    neg = jnp.finfo(jnp.float32).min

    def block(args):
        qblk, i = args
        s = jnp.einsum('bqhcd,bkhcd->bhcqk', qblk, k).astype(jnp.float32)
        qpos = i * Q_BLOCK + jnp.arange(Q_BLOCK)
        mask = kpos[None, :] <= qpos[:, None]
        p = jax.nn.softmax(jnp.where(mask, s, neg), axis=-1)
        w = p[:, :, 0] - lam * p[:, :, 1]
        return jnp.einsum('bhqk,bkhe->bqhe', w.astype(v.dtype), v)

    out = lax.map(block, (qb, jnp.arange(nb)))
    return jnp.swapaxes(out, 0, 1).reshape(B, S, ATTN_HEADS, 2 * DIFF_HEAD_DIM)


def causal_depthwise_conv(x, w, b):
    C = x.shape[-1]
    y = lax.conv_general_dilated(x, w[:, None, :], window_strides=(1,), padding=((CONV_K - 1, 0),),
                                 dimension_numbers=('NWC', 'WIO', 'NWC'), feature_group_count=C)
    return y + b


def spatial_gating(u, v, ln_g, ln_b, w_s, b_s):
    B, S = u.shape[0], u.shape[1]
    v = layer_norm(v, ln_g, ln_b)
    v = v.reshape(B, S // CHUNK, CHUNK, SGU_GROUPS, SGU_GROUP_W)
    causal = jnp.tril(jnp.ones((CHUNK, CHUNK), dtype=w_s.dtype))
    mixed = jnp.einsum('gts,bnsgc->bntgc', w_s * causal, v) + b_s.T[None, None, :, :, None]
    return u * mixed.reshape(B, S, SGU_W)


def hybrid_layer(x, cos, sin, lambda_init, w_in, lam_vecs, subln_g, dw_w, dw_b, cln_g, cln_b,
                 pw_w, pw_b, sln_g, sln_b, sgu_w, sgu_b, w_out, post_g, post_b):
    B, S, _ = x.shape
    h = x @ w_in
    q, k, v, a_gate, c_val, c_glu, c_gate, s_u, s_v, s_gate = jnp.split(h, SPLIT_IDX, axis=-1)

    q = apply_rope(q.reshape(B, S, ATTN_HEADS, 2, DIFF_HEAD_DIM), cos, sin) * (DIFF_HEAD_DIM ** -0.5)
    k = apply_rope(k.reshape(B, S, ATTN_HEADS, 2, DIFF_HEAD_DIM), cos, sin)
    v = v.reshape(B, S, ATTN_HEADS, 2 * DIFF_HEAD_DIM)
    lv = lam_vecs.astype(jnp.float32)
    lam = jnp.exp(jnp.sum(lv[0] * lv[1])) - jnp.exp(jnp.sum(lv[2] * lv[3])) + lambda_init
    o = diff_attention(q, k, v, lam)
    o32 = o.astype(jnp.float32)
    o32 = o32 * lax.rsqrt(jnp.mean(jnp.square(o32), axis=-1, keepdims=True) + RMS_EPS)
    o = (o32 * subln_g.astype(jnp.float32) * (1.0 - lambda_init)).astype(x.dtype)
    attn_out = o.reshape(B, S, ATTN_W) * jax.nn.silu(a_gate)

    c = c_val * jax.nn.sigmoid(c_glu)
    c = causal_depthwise_conv(c, dw_w, dw_b)
    c = jax.nn.silu(layer_norm(c, cln_g, cln_b))
    c = c @ pw_w + pw_b
    conv_out = c * jax.nn.silu(c_gate)

    sg = spatial_gating(jax.nn.gelu(s_u, approximate=False), jax.nn.gelu(s_v, approximate=False),
                        sln_g, sln_b, sgu_w, sgu_b)
    sgu_out = sg * jax.nn.silu(s_gate)

    y = jnp.concatenate([attn_out, conv_out, sgu_out], axis=-1) @ w_out
    return layer_norm(DEEPNORM_ALPHA * x + y, post_g, post_b)


def setup_inputs(seed: int = 0) -> dict:
    key = jax.random.key(seed)
    ks = jax.random.split(key, 18)
    f32 = jnp.float32
    L = DEPTH
    nrm = lambda k, shape, s: jax.random.normal(k, shape, f32) * s
    x = jax.random.normal(ks[0], (BATCH, SEQ, D_MODEL), f32)
    positions = jnp.broadcast_to(jnp.arange(SEQ, dtype=jnp.int32)[None, :], (BATCH, SEQ))
    return {
        "x": x,
        "positions": positions,
        "w_in": nrm(ks[1], (L, D_MODEL, IN_W), D_MODEL ** -0.5),
        "attn_lambda": nrm(ks[2], (L, 4, DIFF_HEAD_DIM), 0.1),
        "attn_subln_g": 1.0 + nrm(ks[3], (L, 2 * DIFF_HEAD_DIM), 0.02),
        "conv_dw_w": nrm(ks[4], (L, CONV_K, CONV_W), CONV_K ** -0.5),
        "conv_dw_b": nrm(ks[5], (L, CONV_W), 0.02),
        "conv_ln_g": 1.0 + nrm(ks[6], (L, CONV_W), 0.02),
        "conv_ln_b": nrm(ks[7], (L, CONV_W), 0.02),
        "conv_pw_w": nrm(ks[8], (L, CONV_W, CONV_W), CONV_W ** -0.5),
        "conv_pw_b": nrm(ks[9], (L, CONV_W), 0.02),
        "sgu_ln_g": 1.0 + nrm(ks[10], (L, SGU_W), 0.02),
        "sgu_ln_b": nrm(ks[11], (L, SGU_W), 0.02),
        "sgu_w": nrm(ks[12], (L, SGU_GROUPS, CHUNK, CHUNK), 0.5 * CHUNK ** -0.5),
        "sgu_b": 1.0 + nrm(ks[13], (L, SGU_GROUPS, CHUNK), 0.02),
        "w_out": nrm(ks[14], (L, D_MODEL, D_MODEL), DEEPNORM_BETA * D_MODEL ** -0.5),
        "post_ln_g": 1.0 + nrm(ks[15], (L, D_MODEL), 0.02),
        "post_ln_b": nrm(ks[16], (L, D_MODEL), 0.02),
    }


def reference(x, positions, w_in, attn_lambda, attn_subln_g, conv_dw_w, conv_dw_b, conv_ln_g,
              conv_ln_b, conv_pw_w, conv_pw_b, sgu_ln_g, sgu_ln_b, sgu_w, sgu_b, w_out,
              post_ln_g, post_ln_b):
    cos, sin = rope_tables(positions, x.dtype)
    for layer in range(DEPTH):
        lambda_init = 0.8 - 0.6 * math.exp(-0.3 * layer)
        x = hybrid_layer(x, cos, sin, lambda_init, w_in[layer], attn_lambda[layer], attn_subln_g[layer],
                         conv_dw_w[layer], conv_dw_b[layer], conv_ln_g[layer], conv_ln_b[layer],
                         conv_pw_w[layer], conv_pw_b[layer], sgu_ln_g[layer], sgu_ln_b[layer],
                         sgu_w[layer], sgu_b[layer], w_out[layer], post_ln_g[layer], post_ln_b[layer])
    return x
```

```python
import functools
import math

import jax
import jax.numpy as jnp
from jax import lax
from jax.experimental import pallas as pl
from jax.experimental.pallas import tpu as pltpu

F32 = jnp.float32
BF16 = jnp.bfloat16

HEAD_DIM = 64
HEAD_W = 2 * HEAD_DIM
CONV_K = 31
CONV_HALO = 32
CHUNK = 128
SGU_GROUP_W = 128
ROPE_THETA = 10000.0
LN_EPS = 1e-5
RMS_EPS = 1e-5
MASK_VALUE = -1e30
VMEM_LIMIT = 56 * 1024 * 1024


def _cparams(*sem):
    return pltpu.CompilerParams(dimension_semantics=sem, vmem_limit_bytes=VMEM_LIMIT)


def _layer_norm_rows(x, g, b):
    mu = jnp.mean(x, axis=-1, keepdims=True)
    xc = x - mu
    var = jnp.mean(xc * xc, axis=-1, keepdims=True)
    return xc * lax.rsqrt(var + LN_EPS) * g + b


def _silu(x):
    return x * jax.nn.sigmoid(x)


def _gelu(x):
    return 0.5 * x * (1.0 + lax.erf(x * (1.0 / math.sqrt(2.0))))


def _rope_body(pos_ref, inv_ref, cos_ref, sin_ref):
    ang = pos_ref[...].astype(F32) * inv_ref[...]
    lane = lax.broadcasted_iota(jnp.int32, ang.shape, 1)
    first_half = (lane % HEAD_DIM) < (HEAD_DIM // 2)
    s = jnp.sin(ang)
    cos_ref[...] = jnp.cos(ang)
    sin_ref[...] = jnp.where(first_half, -s, s)


def _rope_tables(positions):
    S = positions.shape[-1]
    ts = min(S, 1024)
    inv = ROPE_THETA ** (-jnp.arange(0, HEAD_DIM, 2, dtype=F32) / HEAD_DIM)
    inv = jnp.tile(inv, HEAD_W // (HEAD_DIM // 2))[None, :]
    return pl.pallas_call(
        _rope_body,
        grid=(S // ts,),
        in_specs=[pl.BlockSpec((ts, 1), lambda i: (i, 0)),
                  pl.BlockSpec((1, HEAD_W), lambda i: (0, 0))],
        out_specs=[pl.BlockSpec((ts, HEAD_W), lambda i: (i, 0))] * 2,
        out_shape=[jax.ShapeDtypeStruct((S, HEAD_W), F32)] * 2,
        compiler_params=_cparams("parallel"),
        name="rope_tables",
    )(positions.reshape(S, 1), inv)


def _inproj_body(x_ref, w_ref, cos_ref, sin_ref, o_ref, acc_ref, *, unit, rc):
    j = pl.program_id(1)
    tm, tn = acc_ref.shape
    acc_ref[...] = jnp.dot(x_ref[...], w_ref[...], preferred_element_type=F32)

    def in_segs(*segs):
        cond = None
        for s in segs:
            c = (j >= s * unit) & (j < (s + 1) * unit)
            cond = c if cond is None else (cond | c)
        return cond

    def for_row_chunks(fn):
        def body(r, carry):
            fn(pl.ds(pl.multiple_of(r * rc, rc), rc))
            return carry
        lax.fori_loop(0, tm // rc, body, 0)

    def rope(scale):
        def fn(rows):
            cos = cos_ref[rows, :]
            sin = sin_ref[rows, :]
            lane = lax.broadcasted_iota(jnp.int32, (rc, HEAD_W), 1)
            first_half = (lane % HEAD_DIM) < (HEAD_DIM // 2)
            for c0 in range(0, tn, HEAD_W):
                t = acc_ref[rows, c0:c0 + HEAD_W]
                rot = jnp.where(first_half,
                                pltpu.roll(t, HEAD_W - HEAD_DIM // 2, 1),
                                pltpu.roll(t, HEAD_DIM // 2, 1))
                o_ref[rows, c0:c0 + HEAD_W] = ((t * cos + rot * sin) * scale).astype(o_ref.dtype)
        return fn

    def pointwise(act):
        def fn(rows):
            o_ref[rows, :] = act(acc_ref[rows, :]).astype(o_ref.dtype)
        return fn

    @pl.when(in_segs(0, 1))
    def _():
        for_row_chunks(rope(HEAD_DIM ** -0.5))

    @pl.when(in_segs(2, 3))
    def _():
        for_row_chunks(rope(1.0))

    @pl.when(in_segs(4, 5, 8))
    def _():
        for_row_chunks(pointwise(lambda t: t))

    @pl.when(in_segs(6, 7, 10, 13))
    def _():
        for_row_chunks(pointwise(_silu))

    @pl.when(in_segs(9))
    def _():
        for_row_chunks(pointwise(jax.nn.sigmoid))

    @pl.when(in_segs(11, 12))
    def _():
        for_row_chunks(pointwise(_gelu))


def _in_projection(xb, w, cos, sin, *, tm=1024, tn=512):
    S, D = xb.shape
    N = w.shape[1]
    tm = min(tm, S)
    unit = (D // 4) // tn
    return pl.pallas_call(
        functools.partial(_inproj_body, unit=unit, rc=128),
        grid=(S // tm, N // tn),
        in_specs=[pl.BlockSpec((tm, D), lambda i, j: (i, 0)),
                  pl.BlockSpec((D, tn), lambda i, j: (0, j)),
                  pl.BlockSpec((tm, HEAD_W), lambda i, j: (i, 0)),
                  pl.BlockSpec((tm, HEAD_W), lambda i, j: (i, 0))],
        out_specs=pl.BlockSpec((tm, tn), lambda i, j: (i, j)),
        out_shape=jax.ShapeDtypeStruct((S, N), BF16),
        scratch_shapes=[pltpu.VMEM((tm, tn), F32)],
        compiler_params=_cparams("parallel", "arbitrary"),
        name="in_projection",
    )(xb, w, cos, sin)


def _attn_body(lam_ref, g_ref, q_ref, k_ref, v_ref, gate_ref, o_ref, m_ref, l_ref, acc_ref,
               *, tq, lambda_init):
    qi = pl.program_id(1)
    q = q_ref[...]
    lane = lax.broadcasted_iota(jnp.int32, q.shape, 1)
    zero = jnp.zeros_like(q)
    q_halves = (jnp.where(lane < HEAD_DIM, q, zero), jnp.where(lane >= HEAD_DIM, q, zero))
    m_ref[...] = jnp.full(m_ref.shape, MASK_VALUE, F32)
    l_ref[...] = jnp.zeros(l_ref.shape, F32)
    acc_ref[...] = jnp.zeros(acc_ref.shape, F32)

    def step(kb, masked):
        rows = pl.ds(pl.multiple_of(kb * tq, tq), tq)
        k = k_ref[rows, :]
        v = v_ref[rows, :]
        for c in range(2):
            s = lax.dot_general(q_halves[c], k, (((1,), (1,)), ((), ())),
                                preferred_element_type=F32)
            if masked:
                r_id = lax.broadcasted_iota(jnp.int32, s.shape, 0)
                c_id = lax.broadcasted_iota(jnp.int32, s.shape, 1)
                s = jnp.where(c_id <= r_id, s, MASK_VALUE)
            m_prev = m_ref[c]
            m_cur = jnp.maximum(m_prev, jnp.max(s, axis=1, keepdims=True))
            alpha = jnp.exp(m_prev - m_cur)
            p = jnp.exp(s - m_cur)
            l_ref[c] = alpha * l_ref[c] + jnp.sum(p, axis=1, keepdims=True)
            acc_ref[c] = alpha * acc_ref[c] + jnp.dot(p.astype(v.dtype), v,
                                                      preferred_element_type=F32)
            m_ref[c] = m_cur

    def body(kb, carry):
        step(kb, False)
        return carry

    lax.fori_loop(0, qi, body, 0)
    step(qi, True)

    lv = lam_ref[...]
    lam = (jnp.exp(jnp.sum(lv[0:1] * lv[1:2], axis=1, keepdims=True))
           - jnp.exp(jnp.sum(lv[2:3] * lv[3:4], axis=1, keepdims=True)) + lambda_init)
    o = acc_ref[0] / l_ref[0] - lam * (acc_ref[1] / l_ref[1])
    o = o * lax.rsqrt(jnp.mean(o * o, axis=1, keepdims=True) + RMS_EPS)
    o = o * g_ref[...] * (1.0 - lambda_init)
    o_ref[...] = (o * gate_ref[...].astype(F32)).astype(o_ref.dtype)


def _attention(h, lam_vecs, subln_g, *, n_heads, lambda_init, tq=512):
    S = h.shape[0]
    tq = min(tq, S)
    H = n_heads
    return pl.pallas_call(
        functools.partial(_attn_body, tq=tq, lambda_init=lambda_init),
        grid=(H, S // tq),
        in_specs=[pl.BlockSpec((4, HEAD_DIM), lambda hh, i: (0, 0)),
                  pl.BlockSpec((1, HEAD_W), lambda hh, i: (0, 0)),
                  pl.BlockSpec((tq, HEAD_W), lambda hh, i: (i, hh)),
                  pl.BlockSpec((S, HEAD_W), lambda hh, i: (0, H + hh)),
                  pl.BlockSpec((S, HEAD_W), lambda hh, i: (0, 2 * H + hh)),
                  pl.BlockSpec((tq, HEAD_W), lambda hh, i: (i, 3 * H + hh))],
        out_specs=pl.BlockSpec((tq, HEAD_W), lambda hh, i: (i, hh)),
        out_shape=jax.ShapeDtypeStruct((S, H * HEAD_W), BF16),
        scratch_shapes=[pltpu.VMEM((2, tq, 1), F32),
                        pltpu.VMEM((2, tq, 1), F32),
                        pltpu.VMEM((2, tq, HEAD_W), F32)],
        compiler_params=_cparams("parallel", "arbitrary"),
        name="diff_attention",
    )(lam_vecs, subln_g.reshape(1, HEAD_W), h, h, h, h)


def _conv_body(val_ref, glu_ref, pval_ref, pglu_ref, gate_ref, dww_ref, dwb_ref, lng_ref,
               lnb_ref, pww_ref, pwb_ref, o_ref, ext_ref, y_ref, *, rc, cc):
    i = pl.program_id(0)
    T, C = y_ref.shape
    prev = pval_ref[...].astype(F32) * pglu_ref[...].astype(F32)
    ext_ref[0:CONV_HALO, :] = jnp.where(i > 0, prev, 0.0)
    ext_ref[CONV_HALO:CONV_HALO + T, :] = val_ref[...].astype(F32) * glu_ref[...].astype(F32)

    base = CONV_HALO - (CONV_K - 1)
    for r0 in range(0, T, rc):
        for c0 in range(0, C, cc):
            acc = jnp.zeros((rc, cc), F32) + dwb_ref[:, c0:c0 + cc]
            for j in range(CONV_K):
                acc = acc + dww_ref[j:j + 1, c0:c0 + cc] * ext_ref[r0 + base + j:r0 + base + j + rc, c0:c0 + cc]
            y_ref[r0:r0 + rc, c0:c0 + cc] = acc

    a = _silu(_layer_norm_rows(y_ref[...], lng_ref[...], lnb_ref[...]))
    z = jnp.dot(a.astype(BF16), pww_ref[...], preferred_element_type=F32) + pwb_ref[...]
    o_ref[...] = (z * gate_ref[...].astype(F32)).astype(o_ref.dtype)


def _conv_branch(h, dw_w, dw_b, ln_g, ln_b, pw_wb, pw_b, *, seg0, T=256):
    S = h.shape[0]
    C = dw_w.shape[1]
    T = min(T, S)
    hb = T // CONV_HALO
    row = lambda a: a.reshape(1, C)
    prev_map = lambda col: (lambda i: (jnp.maximum(i * hb - 1, 0), col))
    return pl.pallas_call(
        functools.partial(_conv_body, rc=64, cc=256),
        grid=(S // T,),
        in_specs=[pl.BlockSpec((T, C), lambda i: (i, seg0)),
                  pl.BlockSpec((T, C), lambda i: (i, seg0 + 1)),
                  pl.BlockSpec((CONV_HALO, C), prev_map(seg0)),
                  pl.BlockSpec((CONV_HALO, C), prev_map(seg0 + 1)),
                  pl.BlockSpec((T, C), lambda i: (i, seg0 + 2)),
                  pl.BlockSpec((CONV_K, C), lambda i: (0, 0)),
                  pl.BlockSpec((1, C), lambda i: (0, 0)),
                  pl.BlockSpec((1, C), lambda i: (0, 0)),
                  pl.BlockSpec((1, C), lambda i: (0, 0)),
                  pl.BlockSpec((C, C), lambda i: (0, 0)),
                  pl.BlockSpec((1, C), lambda i: (0, 0))],
        out_specs=pl.BlockSpec((T, C), lambda i: (i, 0)),
        out_shape=jax.ShapeDtypeStruct((S, C), BF16),
        scratch_shapes=[pltpu.VMEM((T + CONV_HALO, C), F32),
                        pltpu.VMEM((T, C), F32)],
        compiler_params=_cparams("parallel"),
        name="conv_branch",
    )(h, h, h, h, h, dw_w, row(dw_b), row(ln_g), row(ln_b), pw_wb, row(pw_b))


def _sgu_body(u_ref, v_ref, gate_ref, lng_ref, lnb_ref, w_ref, b_ref, o_ref, vn_ref):
    T, W = vn_ref.shape
    vn_ref[...] = _layer_norm_rows(v_ref[...].astype(F32), lng_ref[...], lnb_ref[...]).astype(BF16)
    r_id = lax.broadcasted_iota(jnp.int32, (CHUNK, CHUNK), 0)
    c_id = lax.broadcasted_iota(jnp.int32, (CHUNK, CHUNK), 1)
    causal = c_id <= r_id
    for g in range(W // SGU_GROUP_W):
        cols = slice(g * SGU_GROUP_W, (g + 1) * SGU_GROUP_W)
        wg = jnp.where(causal, w_ref[g], 0.0).astype(BF16)
        bias = b_ref[g]
        for n in range(T // CHUNK):
            rows = slice(n * CHUNK, (n + 1) * CHUNK)
            mixed = jnp.dot(wg, vn_ref[rows, cols], preferred_element_type=F32) + bias
            o_ref[rows, cols] = (u_ref[rows, cols].astype(F32) * mixed
                                 * gate_ref[rows, cols].astype(F32)).astype(o_ref.dtype)


def _sgu_branch(h, ln_g, ln_b, w_s, b_s, *, seg0, T=512):
    S = h.shape[0]
    G = w_s.shape[0]
    W = G * SGU_GROUP_W
    T = min(T, S)
    return pl.pallas_call(
        _sgu_body,
        grid=(S // T,),
        in_specs=[pl.BlockSpec((T, W), lambda i: (i, seg0)),
                  pl.BlockSpec((T, W), lambda i: (i, seg0 + 1)),
                  pl.BlockSpec((T, W), lambda i: (i, seg0 + 2)),
                  pl.BlockSpec((1, W), lambda i: (0, 0)),
                  pl.BlockSpec((1, W), lambda i: (0, 0)),
                  pl.BlockSpec((G, CHUNK, CHUNK), lambda i: (0, 0, 0)),
                  pl.BlockSpec((G, CHUNK, 1), lambda i: (0, 0, 0))],
        out_specs=pl.BlockSpec((T, W), lambda i: (i, 0)),
        out_shape=jax.ShapeDtypeStruct((S, W), BF16),
        scratch_shapes=[pltpu.VMEM((T, W), BF16)],
        compiler_params=_cparams("parallel"),
        name="sgu_branch",
    )(h, h, h, ln_g.reshape(1, W), ln_b.reshape(1, W), w_s, b_s.reshape(G, CHUNK, 1))


def _outproj_body(a_ref, c_ref, s_ref, wa_ref, wc_ref, ws_ref, x_ref, o_ref, *, alpha):
    y = jnp.dot(a_ref[...], wa_ref[...], preferred_element_type=F32)
    y = y + jnp.dot(c_ref[...], wc_ref[...], preferred_element_type=F32)
    y = y + jnp.dot(s_ref[...], ws_ref[...], preferred_element_type=F32)
    o_ref[...] = alpha * x_ref[...] + y


def _out_projection(attn, conv, sgu, w, x, *, alpha, tm=1024, tn=1024):
    S, D = x.shape
    tm = min(tm, S)
    wa, wc, ws = attn.shape[1], conv.shape[1], sgu.shape[1]
    return pl.pallas_call(
        functools.partial(_outproj_body, alpha=alpha),
        grid=(S // tm, D // tn),
        in_specs=[pl.BlockSpec((tm, wa), lambda i, j: (i, 0)),
                  pl.BlockSpec((tm, wc), lambda i, j: (i, 0)),
                  pl.BlockSpec((tm, ws), lambda i, j: (i, 0)),
                  pl.BlockSpec((wa, tn), lambda i, j: (0, j)),
                  pl.BlockSpec((wc, tn), lambda i, j: (wa // wc, j)),
                  pl.BlockSpec((ws, tn), lambda i, j: ((wa + wc) // ws, j)),
                  pl.BlockSpec((tm, tn), lambda i, j: (i, j))],
        out_specs=pl.BlockSpec((tm, tn), lambda i, j: (i, j)),
        out_shape=jax.ShapeDtypeStruct((S, D), F32),
        compiler_params=_cparams("parallel", "arbitrary"),
        name="out_projection",
    )(attn, conv, sgu, w, w, w, x)


def _postln_body(r_ref, g_ref, b_ref, o_ref, ob_ref):
    y = _layer_norm_rows(r_ref[...], g_ref[...], b_ref[...])
    o_ref[...] = y
    ob_ref[...] = y.astype(ob_ref.dtype)


def _post_layer_norm(r, g, b, *, tr=256):
    S, D = r.shape
    tr = min(tr, S)
    return pl.pallas_call(
        _postln_body,
        grid=(S // tr,),
        in_specs=[pl.BlockSpec((tr, D), lambda i: (i, 0)),
                  pl.BlockSpec((1, D), lambda i: (0, 0)),
                  pl.BlockSpec((1, D), lambda i: (0, 0))],
        out_specs=[pl.BlockSpec((tr, D), lambda i: (i, 0))] * 2,
        out_shape=[jax.ShapeDtypeStruct((S, D), F32), jax.ShapeDtypeStruct((S, D), BF16)],
        compiler_params=_cparams("parallel"),
        name="post_layer_norm",
    )(r, g.reshape(1, D), b.reshape(1, D))


def kernel(x, positions, w_in, attn_lambda, attn_subln_g, conv_dw_w, conv_dw_b, conv_ln_g, conv_ln_b, conv_pw_w, conv_pw_b, sgu_ln_g, sgu_ln_b, sgu_w, sgu_b, w_out, post_ln_g, post_ln_b):
    B, S, D = x.shape
    depth = w_in.shape[0]
    assert B == 1 and D % 4 == 0
    attn_w, conv_w = D // 2, D // 4
    n_heads = attn_w // HEAD_W
    alpha = (2 * depth) ** 0.25
    conv_seg0 = 4 * attn_w // conv_w
    sgu_seg0 = conv_seg0 + 3

    cos, sin = _rope_tables(positions)
    xf = x.reshape(S, D)
    xb = xf.astype(BF16)
    for layer in range(depth):
        lambda_init = 0.8 - 0.6 * math.exp(-0.3 * layer)
        h = _in_projection(xb, w_in[layer].astype(BF16), cos, sin)
        attn = _attention(h, attn_lambda[layer], attn_subln_g[layer],
                          n_heads=n_heads, lambda_init=lambda_init)
        conv = _conv_branch(h, conv_dw_w[layer], conv_dw_b[layer], conv_ln_g[layer],
                            conv_ln_b[layer], conv_pw_w[layer].astype(BF16), conv_pw_b[layer],
                            seg0=conv_seg0)
        sgu = _sgu_branch(h, sgu_ln_g[layer], sgu_ln_b[layer], sgu_w[layer], sgu_b[layer],
                          seg0=sgu_seg0)
        r = _out_projection(attn, conv, sgu, w_out[layer].astype(BF16), xf, alpha=alpha)
        xf, xb = _post_layer_norm(r, post_ln_g[layer], post_ln_b[layer])
    return xf.reshape(B, S, D)
```

```python
import functools
import math

import jax
import jax.numpy as jnp
from jax import lax
from jax.experimental import pallas as pl
from jax.experimental.pallas import tpu as pltpu

F32 = jnp.float32
BF16 = jnp.bfloat16

HEAD_DIM = 64
HEAD_W = 2 * HEAD_DIM
CONV_K = 31
CONV_HALO = 32
CHUNK = 128
SGU_GROUP_W = 128
ROPE_THETA = 10000.0
LN_EPS = 1e-5
RMS_EPS = 1e-5
MASK_VALUE = -1e30
VMEM_LIMIT = 56 * 1024 * 1024


def _cparams(*sem):
    return pltpu.CompilerParams(dimension_semantics=sem, vmem_limit_bytes=VMEM_LIMIT)


def _layer_norm_rows(x, g, b):
    mu = jnp.mean(x, axis=-1, keepdims=True)
    xc = x - mu
    var = jnp.mean(xc * xc, axis=-1, keepdims=True)
    return xc * lax.rsqrt(var + LN_EPS) * g + b


def _silu(x):
    return x * jax.nn.sigmoid(x)


def _gelu(x):
    return 0.5 * x * (1.0 + lax.erf(x * (1.0 / math.sqrt(2.0))))


def _rope_body(pos_ref, inv_ref, cos_ref, sin_ref):
    ang = pos_ref[...].astype(F32) * inv_ref[...]
    lane = lax.broadcasted_iota(jnp.int32, ang.shape, 1)
    first_half = (lane % HEAD_DIM) < (HEAD_DIM // 2)
    s = jnp.sin(ang)
    cos_ref[...] = jnp.cos(ang)
    sin_ref[...] = jnp.where(first_half, -s, s)


def _rope_tables(positions):
    S = positions.shape[-1]
    ts = min(S, 1024)
    inv = ROPE_THETA ** (-jnp.arange(0, HEAD_DIM, 2, dtype=F32) / HEAD_DIM)
    inv = jnp.tile(inv, HEAD_W // (HEAD_DIM // 2))[None, :]
    return pl.pallas_call(
        _rope_body,
        grid=(S // ts,),
        in_specs=[pl.BlockSpec((ts, 1), lambda i: (i, 0)),
                  pl.BlockSpec((1, HEAD_W), lambda i: (0, 0))],
        out_specs=[pl.BlockSpec((ts, HEAD_W), lambda i: (i, 0))] * 2,
        out_shape=[jax.ShapeDtypeStruct((S, HEAD_W), F32)] * 2,
        compiler_params=_cparams("parallel"),
        name="rope_tables",
    )(positions.reshape(S, 1), inv)


def _inproj_body(x_ref, w_ref, cos_ref, sin_ref, o_ref, acc_ref, *, unit, rc):
    j = pl.program_id(1)
    tm, tn = acc_ref.shape
    acc_ref[...] = jnp.dot(x_ref[...], w_ref[...], preferred_element_type=F32)

    def in_segs(*segs):
        cond = None
        for s in segs:
            c = (j >= s * unit) & (j < (s + 1) * unit)
            cond = c if cond is None else (cond | c)
        return cond

    def for_row_chunks(fn):
        def body(r, carry):
            fn(pl.ds(pl.multiple_of(r * rc, rc), rc))
            return carry
        lax.fori_loop(0, tm // rc, body, 0)

    def rope(scale):
        def fn(rows):
            cos = cos_ref[rows, :]
            sin = sin_ref[rows, :]
            lane = lax.broadcasted_iota(jnp.int32, (rc, HEAD_W), 1)
            first_half = (lane % HEAD_DIM) < (HEAD_DIM // 2)
            for c0 in range(0, tn, HEAD_W):
                t = acc_ref[rows, c0:c0 + HEAD_W]
                rot = jnp.where(first_half,
                                pltpu.roll(t, HEAD_W - HEAD_DIM // 2, 1),
                                pltpu.roll(t, HEAD_DIM // 2, 1))
                o_ref[rows, c0:c0 + HEAD_W] = ((t * cos + rot * sin) * scale).astype(o_ref.dtype)
        return fn

    def pointwise(act):
        def fn(rows):
            o_ref[rows, :] = act(acc_ref[rows, :]).astype(o_ref.dtype)
        return fn

    @pl.when(in_segs(0, 1))
    def _():
        for_row_chunks(rope(HEAD_DIM ** -0.5))

    @pl.when(in_segs(2, 3))
    def _():
        for_row_chunks(rope(1.0))

    @pl.when(in_segs(4, 5, 8))
    def _():
        for_row_chunks(pointwise(lambda t: t))

    @pl.when(in_segs(6, 7, 10, 13))
    def _():
        for_row_chunks(pointwise(_silu))

    @pl.when(in_segs(9))
    def _():
        for_row_chunks(pointwise(jax.nn.sigmoid))

    @pl.when(in_segs(11, 12))
    def _():
        for_row_chunks(pointwise(_gelu))


def _in_projection(xb, w, cos, sin, *, tm=1024, tn=512):
    S, D = xb.shape
    N = w.shape[1]
    tm = min(tm, S)
    unit = (D // 4) // tn
    return pl.pallas_call(
        functools.partial(_inproj_body, unit=unit, rc=128),
        grid=(S // tm, N // tn),
        in_specs=[pl.BlockSpec((tm, D), lambda i, j: (i, 0)),
                  pl.BlockSpec((D, tn), lambda i, j: (0, j)),
                  pl.BlockSpec((tm, HEAD_W), lambda i, j: (i, 0)),
                  pl.BlockSpec((tm, HEAD_W), lambda i, j: (i, 0))],
        out_specs=pl.BlockSpec((tm, tn), lambda i, j: (i, j)),
        out_shape=jax.ShapeDtypeStruct((S, N), BF16),
        scratch_shapes=[pltpu.VMEM((tm, tn), F32)],
        compiler_params=_cparams("parallel", "arbitrary"),
        name="in_projection",
    )(xb, w, cos, sin)


def _attn_body(lam_ref, g_ref, q_ref, k_ref, v_ref, gate_ref, o_ref,
               qt_ref, vt_ref, m_ref, l_ref, acc_ref, *, tq, lambda_init):
    qi = pl.program_id(1)
    n_kb = vt_ref.shape[0]

    @pl.when(qi == 0)
    def _():
        for kb in range(n_kb):
            vt_ref[kb] = v_ref[kb * tq:(kb + 1) * tq, :].astype(F32).T.astype(vt_ref.dtype)

    qt = q_ref[...].astype(F32).T
    row = lax.broadcasted_iota(jnp.int32, qt.shape, 0)
    qt_ref[0] = jnp.where(row < HEAD_DIM, qt, 0.0).astype(qt_ref.dtype)
    qt_ref[1] = jnp.where(row >= HEAD_DIM, qt, 0.0).astype(qt_ref.dtype)
    m_ref[...] = jnp.full(m_ref.shape, MASK_VALUE, F32)
    l_ref[...] = jnp.zeros(l_ref.shape, F32)
    acc_ref[...] = jnp.zeros(acc_ref.shape, F32)

    def step(kb, masked):
        k = k_ref[pl.ds(pl.multiple_of(kb * tq, tq), tq), :]
        vt = vt_ref[kb]
        for c in range(2):
            st = jnp.dot(k, qt_ref[c], preferred_element_type=F32)
            if masked:
                k_id = lax.broadcasted_iota(jnp.int32, st.shape, 0)
                q_id = lax.broadcasted_iota(jnp.int32, st.shape, 1)
                st = jnp.where(k_id <= q_id, st, MASK_VALUE)
            m_prev = m_ref[c]
            m_cur = jnp.maximum(m_prev, jnp.max(st, axis=0, keepdims=True))
            alpha = jnp.exp(m_prev - m_cur)
            pt = jnp.exp(st - m_cur)
            l_ref[c] = alpha * l_ref[c] + jnp.sum(pt, axis=0, keepdims=True)
            acc_ref[c] = alpha * acc_ref[c] + jnp.dot(vt, pt.astype(vt.dtype),
                                                      preferred_element_type=F32)
            m_ref[c] = m_cur

    def body(kb, carry):
        step(kb, False)
        return carry

    lax.fori_loop(0, qi, body, 0)
    step(qi, True)

    lv = lam_ref[...]
    lam = (jnp.exp(jnp.sum(lv[0:1] * lv[1:2], axis=1, keepdims=True))
           - jnp.exp(jnp.sum(lv[2:3] * lv[3:4], axis=1, keepdims=True)) + lambda_init)
    ot = acc_ref[0] * (1.0 / l_ref[0]) - lam * (acc_ref[1] * (1.0 / l_ref[1]))
    ot = ot * lax.rsqrt(jnp.mean(ot * ot, axis=0, keepdims=True) + RMS_EPS)
    ot = ot * g_ref[...] * (1.0 - lambda_init)
    o_ref[...] = (ot.T * gate_ref[...].astype(F32)).astype(o_ref.dtype)


def _attention(h, lam_vecs, subln_g, *, n_heads, lambda_init, tq=512):
    S = h.shape[0]
    tq = min(tq, S)
    H = n_heads
    return pl.pallas_call(
        functools.partial(_attn_body, tq=tq, lambda_init=lambda_init),
        grid=(H, S // tq),
        in_specs=[pl.BlockSpec((4, HEAD_DIM), lambda hh, i: (0, 0)),
                  pl.BlockSpec((HEAD_W, 1), lambda hh, i: (0, 0)),
                  pl.BlockSpec((tq, HEAD_W), lambda hh, i: (i, hh)),
                  pl.BlockSpec((S, HEAD_W), lambda hh, i: (0, H + hh)),
                  pl.BlockSpec((S, HEAD_W), lambda hh, i: (0, 2 * H + hh)),
                  pl.BlockSpec((tq, HEAD_W), lambda hh, i: (i, 3 * H + hh))],
        out_specs=pl.BlockSpec((tq, HEAD_W), lambda hh, i: (i, hh)),
        out_shape=jax.ShapeDtypeStruct((S, H * HEAD_W), BF16),
        scratch_shapes=[pltpu.VMEM((2, HEAD_W, tq), BF16),
                        pltpu.VMEM((S // tq, HEAD_W, tq), BF16),
                        pltpu.VMEM((2, 1, tq), F32),
                        pltpu.VMEM((2, 1, tq), F32),
                        pltpu.VMEM((2, HEAD_W, tq), F32)],
        compiler_params=_cparams("arbitrary", "arbitrary"),
        name="diff_attention",
    )(lam_vecs, subln_g.reshape(HEAD_W, 1), h, h, h, h)


def _conv_body(val_ref, glu_ref, pval_ref, pglu_ref, gate_ref, dww_ref, dwb_ref, lng_ref,
               lnb_ref, pww_ref, pwb_ref, o_ref, ext_ref, y_ref, *, rc, cc):
    i = pl.program_id(0)
    T, C = y_ref.shape
    prev = pval_ref[...].astype(F32) * pglu_ref[...].astype(F32)
    ext_ref[0:CONV_HALO, :] = jnp.where(i > 0, prev, 0.0)
    ext_ref[CONV_HALO:CONV_HALO + T, :] = val_ref[...].astype(F32) * glu_ref[...].astype(F32)

    base = CONV_HALO - (CONV_K - 1)
    for r0 in range(0, T, rc):
        for c0 in range(0, C, cc):
            acc = jnp.zeros((rc, cc), F32) + dwb_ref[:, c0:c0 + cc]
            for j in range(CONV_K):
                acc = acc + dww_ref[j:j + 1, c0:c0 + cc] * ext_ref[r0 + base + j:r0 + base + j + rc, c0:c0 + cc]
            y_ref[r0:r0 + rc, c0:c0 + cc] = acc

    a = _silu(_layer_norm_rows(y_ref[...], lng_ref[...], lnb_ref[...]))
    z = jnp.dot(a.astype(BF16), pww_ref[...], preferred_element_type=F32) + pwb_ref[...]
    o_ref[...] = (z * gate_ref[...].astype(F32)).astype(o_ref.dtype)


def _conv_branch(h, dw_w, dw_b, ln_g, ln_b, pw_wb, pw_b, *, seg0, T=256):
    S = h.shape[0]
    C = dw_w.shape[1]
    T = min(T, S)
    hb = T // CONV_HALO
    row = lambda a: a.reshape(1, C)
    prev_map = lambda col: (lambda i: (jnp.maximum(i * hb - 1, 0), col))
    return pl.pallas_call(
        functools.partial(_conv_body, rc=64, cc=256),
        grid=(S // T,),
        in_specs=[pl.BlockSpec((T, C), lambda i: (i, seg0)),
                  pl.BlockSpec((T, C), lambda i: (i, seg0 + 1)),
                  pl.BlockSpec((CONV_HALO, C), prev_map(seg0)),
                  pl.BlockSpec((CONV_HALO, C), prev_map(seg0 + 1)),
                  pl.BlockSpec((T, C), lambda i: (i, seg0 + 2)),
                  pl.BlockSpec((CONV_K, C), lambda i: (0, 0)),
                  pl.BlockSpec((1, C), lambda i: (0, 0)),
                  pl.BlockSpec((1, C), lambda i: (0, 0)),
                  pl.BlockSpec((1, C), lambda i: (0, 0)),
                  pl.BlockSpec((C, C), lambda i: (0, 0)),
                  pl.BlockSpec((1, C), lambda i: (0, 0))],
        out_specs=pl.BlockSpec((T, C), lambda i: (i, 0)),
        out_shape=jax.ShapeDtypeStruct((S, C), BF16),
        scratch_shapes=[pltpu.VMEM((T + CONV_HALO, C), F32),
                        pltpu.VMEM((T, C), F32)],
        compiler_params=_cparams("parallel"),
        name="conv_branch",
    )(h, h, h, h, h, dw_w, row(dw_b), row(ln_g), row(ln_b), pw_wb, row(pw_b))


def _sgu_body(u_ref, v_ref, gate_ref, lng_ref, lnb_ref, w_ref, b_ref, o_ref, vn_ref):
    T, W = vn_ref.shape
    vn_ref[...] = _layer_norm_rows(v_ref[...].astype(F32), lng_ref[...], lnb_ref[...]).astype(BF16)
    r_id = lax.broadcasted_iota(jnp.int32, (CHUNK, CHUNK), 0)
    c_id = lax.broadcasted_iota(jnp.int32, (CHUNK, CHUNK), 1)
    causal = c_id <= r_id
    for g in range(W // SGU_GROUP_W):
        cols = slice(g * SGU_GROUP_W, (g + 1) * SGU_GROUP_W)
        wg = jnp.where(causal, w_ref[g], 0.0).astype(BF16)
        bias = b_ref[g]
        for n in range(T // CHUNK):
            rows = slice(n * CHUNK, (n + 1) * CHUNK)
            mixed = jnp.dot(wg, vn_ref[rows, cols], preferred_element_type=F32) + bias
            o_ref[rows, cols] = (u_ref[rows, cols].astype(F32) * mixed
                                 * gate_ref[rows, cols].astype(F32)).astype(o_ref.dtype)


def _sgu_branch(h, ln_g, ln_b, w_s, b_s, *, seg0, T=512):
    S = h.shape[0]
    G = w_s.shape[0]
    W = G * SGU_GROUP_W
    T = min(T, S)
    return pl.pallas_call(
        _sgu_body,
        grid=(S // T,),
        in_specs=[pl.BlockSpec((T, W), lambda i: (i, seg0)),
                  pl.BlockSpec((T, W), lambda i: (i, seg0 + 1)),
                  pl.BlockSpec((T, W), lambda i: (i, seg0 + 2)),
                  pl.BlockSpec((1, W), lambda i: (0, 0)),
                  pl.BlockSpec((1, W), lambda i: (0, 0)),
                  pl.BlockSpec((G, CHUNK, CHUNK), lambda i: (0, 0, 0)),
                  pl.BlockSpec((G, CHUNK, 1), lambda i: (0, 0, 0))],
        out_specs=pl.BlockSpec((T, W), lambda i: (i, 0)),
        out_shape=jax.ShapeDtypeStruct((S, W), BF16),
        scratch_shapes=[pltpu.VMEM((T, W), BF16)],
        compiler_params=_cparams("parallel"),
        name="sgu_branch",
    )(h, h, h, ln_g.reshape(1, W), ln_b.reshape(1, W), w_s, b_s.reshape(G, CHUNK, 1))


def _outproj_body(a_ref, c_ref, s_ref, wa_ref, wc_ref, ws_ref, x_ref, o_ref, *, alpha):
    y = jnp.dot(a_ref[...], wa_ref[...], preferred_element_type=F32)
    y = y + jnp.dot(c_ref[...], wc_ref[...], preferred_element_type=F32)
    y = y + jnp.dot(s_ref[...], ws_ref[...], preferred_element_type=F32)
    o_ref[...] = alpha * x_ref[...] + y


def _out_projection(attn, conv, sgu, w, x, *, alpha, tm=1024, tn=1024):
    S, D = x.shape
    tm = min(tm, S)
    wa, wc, ws = attn.shape[1], conv.shape[1], sgu.shape[1]
    return pl.pallas_call(
        functools.partial(_outproj_body, alpha=alpha),
        grid=(S // tm, D // tn),
        in_specs=[pl.BlockSpec((tm, wa), lambda i, j: (i, 0)),
                  pl.BlockSpec((tm, wc), lambda i, j: (i, 0)),
                  pl.BlockSpec((tm, ws), lambda i, j: (i, 0)),
                  pl.BlockSpec((wa, tn), lambda i, j: (0, j)),
                  pl.BlockSpec((wc, tn), lambda i, j: (wa // wc, j)),
                  pl.BlockSpec((ws, tn), lambda i, j: ((wa + wc) // ws, j)),
                  pl.BlockSpec((tm, tn), lambda i, j: (i, j))],
        out_specs=pl.BlockSpec((tm, tn), lambda i, j: (i, j)),
        out_shape=jax.ShapeDtypeStruct((S, D), F32),
        compiler_params=_cparams("parallel", "arbitrary"),
        name="out_projection",
    )(attn, conv, sgu, w, w, w, x)


def _postln_body(r_ref, g_ref, b_ref, o_ref, ob_ref):
    y = _layer_norm_rows(r_ref[...], g_ref[...], b_ref[...])
    o_ref[...] = y
    ob_ref[...] = y.astype(ob_ref.dtype)


def _post_layer_norm(r, g, b, *, tr=256):
    S, D = r.shape
    tr = min(tr, S)
    return pl.pallas_call(
        _postln_body,
        grid=(S // tr,),
        in_specs=[pl.BlockSpec((tr, D), lambda i: (i, 0)),
                  pl.BlockSpec((1, D), lambda i: (0, 0)),
                  pl.BlockSpec((1, D), lambda i: (0, 0))],
        out_specs=[pl.BlockSpec((tr, D), lambda i: (i, 0))] * 2,
        out_shape=[jax.ShapeDtypeStruct((S, D), F32), jax.ShapeDtypeStruct((S, D), BF16)],
        compiler_params=_cparams("parallel"),
        name="post_layer_norm",
    )(r, g.reshape(1, D), b.reshape(1, D))


def kernel(x, positions, w_in, attn_lambda, attn_subln_g, conv_dw_w, conv_dw_b, conv_ln_g, conv_ln_b, conv_pw_w, conv_pw_b, sgu_ln_g, sgu_ln_b, sgu_w, sgu_b, w_out, post_ln_g, post_ln_b):
    B, S, D = x.shape
    depth = w_in.shape[0]
    assert B == 1 and D % 4 == 0
    attn_w, conv_w = D // 2, D // 4
    n_heads = attn_w // HEAD_W
    alpha = (2 * depth) ** 0.25
    conv_seg0 = 4 * attn_w // conv_w
    sgu_seg0 = conv_seg0 + 3

    cos, sin = _rope_tables(positions)
    xf = x.reshape(S, D)
    xb = xf.astype(BF16)
    for layer in range(depth):
        lambda_init = 0.8 - 0.6 * math.exp(-0.3 * layer)
        h = _in_projection(xb, w_in[layer].astype(BF16), cos, sin)
        attn = _attention(h, attn_lambda[layer], attn_subln_g[layer],
                          n_heads=n_heads, lambda_init=lambda_init)
        conv = _conv_branch(h, conv_dw_w[layer], conv_dw_b[layer], conv_ln_g[layer],
                            conv_ln_b[layer], conv_pw_w[layer].astype(BF16), conv_pw_b[layer],
                            seg0=conv_seg0)
        sgu = _sgu_branch(h, sgu_ln_g[layer], sgu_ln_b[layer], sgu_w[layer], sgu_b[layer],
                          seg0=sgu_seg0)
        r = _out_projection(attn, conv, sgu, w_out[layer].astype(BF16), xf, alpha=alpha)
        xf, xb = _post_layer_norm(r, post_ln_g[layer], post_ln_b[layer])
    return xf.reshape(B, S, D)
```

```python
import functools
import math

import jax
import jax.numpy as jnp
from jax import lax
from jax.experimental import pallas as pl
from jax.experimental.pallas import tpu as pltpu

F32 = jnp.float32
BF16 = jnp.bfloat16

HEAD_DIM = 64
HEAD_W = 2 * HEAD_DIM
CONV_K = 31
CONV_HALO = 32
CHUNK = 128
SGU_GROUP_W = 128
ROPE_THETA = 10000.0
LN_EPS = 1e-5
RMS_EPS = 1e-5
MASK_VALUE = -1e30
Q_SCALE = HEAD_DIM ** -0.5 * math.log2(math.e)
SUBLANES = 8
VMEM_LIMIT = 56 * 1024 * 1024


def _cparams(*sem):
    return pltpu.CompilerParams(dimension_semantics=sem, vmem_limit_bytes=VMEM_LIMIT)


def _layer_norm_rows(x, g, b):
    mu = jnp.mean(x, axis=-1, keepdims=True)
    xc = x - mu
    var = jnp.mean(xc * xc, axis=-1, keepdims=True)
    return xc * lax.rsqrt(var + LN_EPS) * g + b


def _silu(x):
    return x * jax.nn.sigmoid(x)


def _gelu(x):
    return 0.5 * x * (1.0 + lax.erf(x * (1.0 / math.sqrt(2.0))))


def _rope_body(pos_ref, inv_ref, cos_ref, sin_ref):
    ang = pos_ref[...].astype(F32) * inv_ref[...]
    lane = lax.broadcasted_iota(jnp.int32, ang.shape, 1)
    first_half = (lane % HEAD_DIM) < (HEAD_DIM // 2)
    s = jnp.sin(ang)
    cos_ref[...] = jnp.cos(ang)
    sin_ref[...] = jnp.where(first_half, -s, s)


def _rope_tables(positions):
    S = positions.shape[-1]
    ts = min(S, 1024)
    inv = ROPE_THETA ** (-jnp.arange(0, HEAD_DIM, 2, dtype=F32) / HEAD_DIM)
    inv = jnp.tile(inv, HEAD_W // (HEAD_DIM // 2))[None, :]
    return pl.pallas_call(
        _rope_body,
        grid=(S // ts,),
        in_specs=[pl.BlockSpec((ts, 1), lambda i: (i, 0)),
                  pl.BlockSpec((1, HEAD_W), lambda i: (0, 0))],
        out_specs=[pl.BlockSpec((ts, HEAD_W), lambda i: (i, 0))] * 2,
        out_shape=[jax.ShapeDtypeStruct((S, HEAD_W), F32)] * 2,
        compiler_params=_cparams("parallel"),
        name="rope_tables",
    )(positions.reshape(S, 1), inv)


def _inproj_body(x_ref, w_ref, cos_ref, sin_ref, o_ref, acc_ref, *, unit, rc):
    j = pl.program_id(1)
    tm, tn = acc_ref.shape
    acc_ref[...] = jnp.dot(x_ref[...], w_ref[...], preferred_element_type=F32)

    def in_segs(*segs):
        cond = None
        for s in segs:
            c = (j >= s * unit) & (j < (s + 1) * unit)
            cond = c if cond is None else (cond | c)
        return cond

    def for_row_chunks(fn):
        def body(r, carry):
            fn(pl.ds(pl.multiple_of(r * rc, rc), rc))
            return carry
        lax.fori_loop(0, tm // rc, body, 0)

    def rope(scale):
        def fn(rows):
            cos = cos_ref[rows, :]
            sin = sin_ref[rows, :]
            lane = lax.broadcasted_iota(jnp.int32, (rc, HEAD_W), 1)
            first_half = (lane % HEAD_DIM) < (HEAD_DIM // 2)
            for c0 in range(0, tn, HEAD_W):
                t = acc_ref[rows, c0:c0 + HEAD_W]
                rot = jnp.where(first_half,
                                pltpu.roll(t, HEAD_W - HEAD_DIM // 2, 1),
                                pltpu.roll(t, HEAD_DIM // 2, 1))
                o_ref[rows, c0:c0 + HEAD_W] = ((t * cos + rot * sin) * scale).astype(o_ref.dtype)
        return fn

    def pointwise(act):
        def fn(rows):
            o_ref[rows, :] = act(acc_ref[rows, :]).astype(o_ref.dtype)
        return fn

    @pl.when(in_segs(0, 1))
    def _():
        for_row_chunks(rope(Q_SCALE))

    @pl.when(in_segs(2, 3))
    def _():
        for_row_chunks(rope(1.0))

    @pl.when(in_segs(4, 5, 8))
    def _():
        for_row_chunks(pointwise(lambda t: t))

    @pl.when(in_segs(6, 7, 10, 13))
    def _():
        for_row_chunks(pointwise(_silu))

    @pl.when(in_segs(9))
    def _():
        for_row_chunks(pointwise(jax.nn.sigmoid))

    @pl.when(in_segs(11, 12))
    def _():
        for_row_chunks(pointwise(_gelu))


def _in_projection(xb, w, cos, sin, *, tm=1024, tn=512):
    S, D = xb.shape
    N = w.shape[1]
    tm = min(tm, S)
    unit = (D // 4) // tn
    return pl.pallas_call(
        functools.partial(_inproj_body, unit=unit, rc=128),
        grid=(S // tm, N // tn),
        in_specs=[pl.BlockSpec((tm, D), lambda i, j: (i, 0)),
                  pl.BlockSpec((D, tn), lambda i, j: (0, j)),
                  pl.BlockSpec((tm, HEAD_W), lambda i, j: (i, 0)),
                  pl.BlockSpec((tm, HEAD_W), lambda i, j: (i, 0))],
        out_specs=pl.BlockSpec((tm, tn), lambda i, j: (i, j)),
        out_shape=jax.ShapeDtypeStruct((S, N), BF16),
        scratch_shapes=[pltpu.VMEM((tm, tn), F32)],
        compiler_params=_cparams("parallel", "arbitrary"),
        name="in_projection",
    )(xb, w, cos, sin)


def _attn_body(lam_ref, g_ref, q_ref, k_ref, v_ref, gate_ref, o_ref,
               qt_ref, vt_ref, st_ref, bm_ref, pt_ref, al_ref, m_ref, l_ref, acc_ref,
               *, tq, rc, lambda_init):
    qi = pl.program_id(1)
    n_kb = vt_ref.shape[0]

    @pl.when(qi == 0)
    def _():
        for kb in range(n_kb):
            vt_ref[kb] = v_ref[kb * tq:(kb + 1) * tq, :].astype(F32).T.astype(vt_ref.dtype)

    qt = q_ref[...].astype(F32).T
    row = lax.broadcasted_iota(jnp.int32, qt.shape, 0)
    qt_ref[0] = jnp.where(row < HEAD_DIM, qt, 0.0).astype(qt_ref.dtype)
    qt_ref[1] = jnp.where(row >= HEAD_DIM, qt, 0.0).astype(qt_ref.dtype)
    m_ref[...] = jnp.full(m_ref.shape, MASK_VALUE, F32)
    l_ref[...] = jnp.zeros(l_ref.shape, F32)
    acc_ref[...] = jnp.zeros(acc_ref.shape, F32)

    halves = range(2)

    n_ch = tq // rc
    kc = 2 * rc

    def by_vreg(a):
        return a.reshape(a.shape[0] // SUBLANES, SUBLANES, a.shape[1])

    def score_chunk(b, slot, r, bm):
        k = k_ref[pl.ds(pl.multiple_of(b * tq + r * rc, rc), rc), :]
        out = []
        for c in halves:
            st = jnp.dot(k, qt_ref[c], preferred_element_type=F32)
            st_ref[slot, c, r * rc:(r + 1) * rc, :] = st
            cm = jnp.max(by_vreg(st), axis=0)
            out.append(cm if bm is None else jnp.maximum(bm[c], cm))
        return out

    def score_end(slot, bm):
        for c in halves:
            bm_ref[slot, c] = jnp.max(bm[c], axis=0, keepdims=True)

    def load_scores(slot, c, r, masked):
        st = st_ref[slot, c, r * rc:(r + 1) * rc, :]
        if masked:
            k_id = lax.broadcasted_iota(jnp.int32, st.shape, 0) + r * rc
            q_id = lax.broadcasted_iota(jnp.int32, st.shape, 1)
            st = jnp.where(k_id <= q_id, st, MASK_VALUE)
        return st

    def prob_begin(slot, masked):
        if masked:
            bmax = []
            for c in halves:
                bm = jnp.max(load_scores(slot, c, 0, True), axis=0, keepdims=True)
                for r in range(1, n_ch):
                    bm = jnp.maximum(bm, jnp.max(load_scores(slot, c, r, True), axis=0, keepdims=True))
                bmax.append(bm)
        else:
            bmax = [bm_ref[slot, c] for c in halves]
        m_prev = [m_ref[c] for c in halves]
        m_cur = [jnp.maximum(m_prev[c], bmax[c]) for c in halves]
        alpha = [jnp.exp2(m_prev[c] - m_cur[c]) for c in halves]
        return m_cur, alpha

    def prob_chunk(slot, r, masked, m_cur, psum):
        out = []
        for c in halves:
            p = jnp.exp2(load_scores(slot, c, r, masked) - m_cur[c])
            pt_ref[slot, c, r * rc:(r + 1) * rc, :] = p.astype(pt_ref.dtype)
            ps = jnp.sum(by_vreg(p), axis=0)
            out.append(ps if psum is None else psum[c] + ps)
        return out

    def prob_end(slot, m_cur, alpha, psum):
        for c in halves:
            al_ref[slot, c] = alpha[c]
            l_ref[c] = alpha[c] * l_ref[c] + jnp.sum(psum[c], axis=0, keepdims=True)
            m_ref[c] = m_cur[c]

    def value_chunk(b, slot, j, part):
        cols = slice(j * kc, (j + 1) * kc)
        vt = vt_ref[b, :, cols]
        out = []
        for c in halves:
            pv = jnp.dot(vt, pt_ref[slot, c, cols, :], preferred_element_type=F32)
            out.append(pv if part is None else part[c] + pv)
        return out

    def value_end(slot, part):
        for c in halves:
            acc_ref[c] = al_ref[slot, c] * acc_ref[c] + part[c]

    def score(b, slot):
        bm = None
        for r in range(n_ch):
            bm = score_chunk(b, slot, r, bm)
        score_end(slot, bm)

    def prob(slot, masked):
        m_cur, alpha = prob_begin(slot, masked)
        psum = None
        for r in range(n_ch):
            psum = prob_chunk(slot, r, masked, m_cur, psum)
        prob_end(slot, m_cur, alpha, psum)

    def value(b, slot):
        part = None
        for j in range(tq // kc):
            part = value_chunk(b, slot, j, part)
        value_end(slot, part)

    score(0, 0)

    def trip(b, par):
        m_cur, alpha = prob_begin(par, False)
        bm = psum = part = None
        for r in range(n_ch):
            bm = score_chunk(b + 1, 1 - par, r, bm)
            psum = prob_chunk(par, r, False, m_cur, psum)
            if r % 2 == 1:
                part = value_chunk(b, par, r // 2, part)
        score_end(1 - par, bm)
        prob_end(par, m_cur, alpha, psum)
        value_end(par, part)

    def body(b, carry):
        lax.cond(b % 2 == 0, lambda: trip(b, 0), lambda: trip(b, 1))
        return carry

    lax.fori_loop(0, qi, body, 0)

    for par in range(2):
        @pl.when(qi % 2 == par)
        def _():
            prob(par, True)
            value(qi, par)

    lv = lam_ref[...]
    lam = (jnp.exp(jnp.sum(lv[0:1] * lv[1:2], axis=1, keepdims=True))
           - jnp.exp(jnp.sum(lv[2:3] * lv[3:4], axis=1, keepdims=True)) + lambda_init)
    ot = acc_ref[0] * (1.0 / l_ref[0]) - lam * (acc_ref[1] * (1.0 / l_ref[1]))
    ot = ot * lax.rsqrt(jnp.mean(ot * ot, axis=0, keepdims=True) + RMS_EPS)
    ot = ot * g_ref[...] * (1.0 - lambda_init)
    o_ref[...] = (ot.T * gate_ref[...].astype(F32)).astype(o_ref.dtype)


def _attention(h, lam_vecs, subln_g, *, n_heads, lambda_init, tq=512):
    S = h.shape[0]
    tq = min(tq, S)
    H = n_heads
    return pl.pallas_call(
        functools.partial(_attn_body, tq=tq, rc=min(128, tq // 2), lambda_init=lambda_init),
        grid=(H, S // tq),
        in_specs=[pl.BlockSpec((4, HEAD_DIM), lambda hh, i: (0, 0)),
                  pl.BlockSpec((HEAD_W, 1), lambda hh, i: (0, 0)),
                  pl.BlockSpec((tq, HEAD_W), lambda hh, i: (i, hh)),
                  pl.BlockSpec((S, HEAD_W), lambda hh, i: (0, H + hh)),
                  pl.BlockSpec((S, HEAD_W), lambda hh, i: (0, 2 * H + hh)),
                  pl.BlockSpec((tq, HEAD_W), lambda hh, i: (i, 3 * H + hh))],
        out_specs=pl.BlockSpec((tq, HEAD_W), lambda hh, i: (i, hh)),
        out_shape=jax.ShapeDtypeStruct((S, H * HEAD_W), BF16),
        scratch_shapes=[pltpu.VMEM((2, HEAD_W, tq), BF16),
                        pltpu.VMEM((S // tq, HEAD_W, tq), BF16),
                        pltpu.VMEM((2, 2, tq, tq), F32),
                        pltpu.VMEM((2, 2, 1, tq), F32),
                        pltpu.VMEM((2, 2, tq, tq), BF16),
                        pltpu.VMEM((2, 2, 1, tq), F32),
                        pltpu.VMEM((2, 1, tq), F32),
                        pltpu.VMEM((2, 1, tq), F32),
                        pltpu.VMEM((2, HEAD_W, tq), F32)],
        compiler_params=_cparams("arbitrary", "arbitrary"),
        name="diff_attention",
    )(lam_vecs, subln_g.reshape(HEAD_W, 1), h, h, h, h)


def _conv_body(val_ref, glu_ref, pval_ref, pglu_ref, gate_ref, dww_ref, dwb_ref, lng_ref,
               lnb_ref, pww_ref, pwb_ref, o_ref, ext_ref, y_ref, *, rc, cc):
    i = pl.program_id(0)
    T, C = y_ref.shape
    prev = pval_ref[...].astype(F32) * pglu_ref[...].astype(F32)
    ext_ref[0:CONV_HALO, :] = jnp.where(i > 0, prev, 0.0)
    ext_ref[CONV_HALO:CONV_HALO + T, :] = val_ref[...].astype(F32) * glu_ref[...].astype(F32)

    base = CONV_HALO - (CONV_K - 1)
    for r0 in range(0, T, rc):
        for c0 in range(0, C, cc):
            acc = jnp.zeros((rc, cc), F32) + dwb_ref[:, c0:c0 + cc]
            for j in range(CONV_K):
                acc = acc + dww_ref[j:j + 1, c0:c0 + cc] * ext_ref[r0 + base + j:r0 + base + j + rc, c0:c0 + cc]
            y_ref[r0:r0 + rc, c0:c0 + cc] = acc

    a = _silu(_layer_norm_rows(y_ref[...], lng_ref[...], lnb_ref[...]))
    z = jnp.dot(a.astype(BF16), pww_ref[...], preferred_element_type=F32) + pwb_ref[...]
    o_ref[...] = (z * gate_ref[...].astype(F32)).astype(o_ref.dtype)


def _conv_branch(h, dw_w, dw_b, ln_g, ln_b, pw_wb, pw_b, *, seg0, T=256):
    S = h.shape[0]
    C = dw_w.shape[1]
    T = min(T, S)
    hb = T // CONV_HALO
    row = lambda a: a.reshape(1, C)
    prev_map = lambda col: (lambda i: (jnp.maximum(i * hb - 1, 0), col))
    return pl.pallas_call(
        functools.partial(_conv_body, rc=64, cc=256),
        grid=(S // T,),
        in_specs=[pl.BlockSpec((T, C), lambda i: (i, seg0)),
                  pl.BlockSpec((T, C), lambda i: (i, seg0 + 1)),
                  pl.BlockSpec((CONV_HALO, C), prev_map(seg0)),
                  pl.BlockSpec((CONV_HALO, C), prev_map(seg0 + 1)),
                  pl.BlockSpec((T, C), lambda i: (i, seg0 + 2)),
                  pl.BlockSpec((CONV_K, C), lambda i: (0, 0)),
                  pl.BlockSpec((1, C), lambda i: (0, 0)),
                  pl.BlockSpec((1, C), lambda i: (0, 0)),
                  pl.BlockSpec((1, C), lambda i: (0, 0)),
                  pl.BlockSpec((C, C), lambda i: (0, 0)),
                  pl.BlockSpec((1, C), lambda i: (0, 0))],
        out_specs=pl.BlockSpec((T, C), lambda i: (i, 0)),
        out_shape=jax.ShapeDtypeStruct((S, C), BF16),
        scratch_shapes=[pltpu.VMEM((T + CONV_HALO, C), F32),
                        pltpu.VMEM((T, C), F32)],
        compiler_params=_cparams("parallel"),
        name="conv_branch",
    )(h, h, h, h, h, dw_w, row(dw_b), row(ln_g), row(ln_b), pw_wb, row(pw_b))


def _sgu_body(u_ref, v_ref, gate_ref, lng_ref, lnb_ref, w_ref, b_ref, o_ref, vn_ref):
    T, W = vn_ref.shape
    vn_ref[...] = _layer_norm_rows(v_ref[...].astype(F32), lng_ref[...], lnb_ref[...]).astype(BF16)
    r_id = lax.broadcasted_iota(jnp.int32, (CHUNK, CHUNK), 0)
    c_id = lax.broadcasted_iota(jnp.int32, (CHUNK, CHUNK), 1)
    causal = c_id <= r_id
    for g in range(W // SGU_GROUP_W):
        cols = slice(g * SGU_GROUP_W, (g + 1) * SGU_GROUP_W)
        wg = jnp.where(causal, w_ref[g], 0.0).astype(BF16)
        bias = b_ref[g]
        for n in range(T // CHUNK):
            rows = slice(n * CHUNK, (n + 1) * CHUNK)
            mixed = jnp.dot(wg, vn_ref[rows, cols], preferred_element_type=F32) + bias
            o_ref[rows, cols] = (u_ref[rows, cols].astype(F32) * mixed
                                 * gate_ref[rows, cols].astype(F32)).astype(o_ref.dtype)


def _sgu_branch(h, ln_g, ln_b, w_s, b_s, *, seg0, T=512):
    S = h.shape[0]
    G = w_s.shape[0]
    W = G * SGU_GROUP_W
    T = min(T, S)
    return pl.pallas_call(
        _sgu_body,
        grid=(S // T,),
        in_specs=[pl.BlockSpec((T, W), lambda i: (i, seg0)),
                  pl.BlockSpec((T, W), lambda i: (i, seg0 + 1)),
                  pl.BlockSpec((T, W), lambda i: (i, seg0 + 2)),
                  pl.BlockSpec((1, W), lambda i: (0, 0)),
                  pl.BlockSpec((1, W), lambda i: (0, 0)),
                  pl.BlockSpec((G, CHUNK, CHUNK), lambda i: (0, 0, 0)),
                  pl.BlockSpec((G, CHUNK, 1), lambda i: (0, 0, 0))],
        out_specs=pl.BlockSpec((T, W), lambda i: (i, 0)),
        out_shape=jax.ShapeDtypeStruct((S, W), BF16),
        scratch_shapes=[pltpu.VMEM((T, W), BF16)],
        compiler_params=_cparams("parallel"),
        name="sgu_branch",
    )(h, h, h, ln_g.reshape(1, W), ln_b.reshape(1, W), w_s, b_s.reshape(G, CHUNK, 1))


def _outproj_body(a_ref, c_ref, s_ref, wa_ref, wc_ref, ws_ref, x_ref, o_ref, *, alpha):
    y = jnp.dot(a_ref[...], wa_ref[...], preferred_element_type=F32)
    y = y + jnp.dot(c_ref[...], wc_ref[...], preferred_element_type=F32)
    y = y + jnp.dot(s_ref[...], ws_ref[...], preferred_element_type=F32)
    o_ref[...] = alpha * x_ref[...] + y


def _out_projection(attn, conv, sgu, w, x, *, alpha, tm=1024, tn=1024):
    S, D = x.shape
    tm = min(tm, S)
    wa, wc, ws = attn.shape[1], conv.shape[1], sgu.shape[1]
    return pl.pallas_call(
        functools.partial(_outproj_body, alpha=alpha),
        grid=(S // tm, D // tn),
        in_specs=[pl.BlockSpec((tm, wa), lambda i, j: (i, 0)),
                  pl.BlockSpec((tm, wc), lambda i, j: (i, 0)),
                  pl.BlockSpec((tm, ws), lambda i, j: (i, 0)),
                  pl.BlockSpec((wa, tn), lambda i, j: (0, j)),
                  pl.BlockSpec((wc, tn), lambda i, j: (wa // wc, j)),
                  pl.BlockSpec((ws, tn), lambda i, j: ((wa + wc) // ws, j)),
                  pl.BlockSpec((tm, tn), lambda i, j: (i, j))],
        out_specs=pl.BlockSpec((tm, tn), lambda i, j: (i, j)),
        out_shape=jax.ShapeDtypeStruct((S, D), F32),
        compiler_params=_cparams("parallel", "arbitrary"),
        name="out_projection",
    )(attn, conv, sgu, w, w, w, x)


def _postln_body(r_ref, g_ref, b_ref, o_ref, ob_ref):
    y = _layer_norm_rows(r_ref[...], g_ref[...], b_ref[...])
    o_ref[...] = y
    ob_ref[...] = y.astype(ob_ref.dtype)


def _post_layer_norm(r, g, b, *, tr=256):
    S, D = r.shape
    tr = min(tr, S)
    return pl.pallas_call(
        _postln_body,
        grid=(S // tr,),
        in_specs=[pl.BlockSpec((tr, D), lambda i: (i, 0)),
                  pl.BlockSpec((1, D), lambda i: (0, 0)),
                  pl.BlockSpec((1, D), lambda i: (0, 0))],
        out_specs=[pl.BlockSpec((tr, D), lambda i: (i, 0))] * 2,
        out_shape=[jax.ShapeDtypeStruct((S, D), F32), jax.ShapeDtypeStruct((S, D), BF16)],
        compiler_params=_cparams("parallel"),
        name="post_layer_norm",
    )(r, g.reshape(1, D), b.reshape(1, D))


def kernel(x, positions, w_in, attn_lambda, attn_subln_g, conv_dw_w, conv_dw_b, conv_ln_g, conv_ln_b, conv_pw_w, conv_pw_b, sgu_ln_g, sgu_ln_b, sgu_w, sgu_b, w_out, post_ln_g, post_ln_b):
    B, S, D = x.shape
    depth = w_in.shape[0]
    assert B == 1 and D % 4 == 0
    attn_w, conv_w = D // 2, D // 4
    n_heads = attn_w // HEAD_W
    alpha = (2 * depth) ** 0.25
    conv_seg0 = 4 * attn_w // conv_w
    sgu_seg0 = conv_seg0 + 3

    cos, sin = _rope_tables(positions)
    xf = x.reshape(S, D)
    xb = xf.astype(BF16)
    for layer in range(depth):
        lambda_init = 0.8 - 0.6 * math.exp(-0.3 * layer)
        h = _in_projection(xb, w_in[layer].astype(BF16), cos, sin)
        attn = _attention(h, attn_lambda[layer], attn_subln_g[layer],
                          n_heads=n_heads, lambda_init=lambda_init)
        conv = _conv_branch(h, conv_dw_w[layer], conv_dw_b[layer], conv_ln_g[layer],
                            conv_ln_b[layer], conv_pw_w[layer].astype(BF16), conv_pw_b[layer],
                            seg0=conv_seg0)
        sgu = _sgu_branch(h, sgu_ln_g[layer], sgu_ln_b[layer], sgu_w[layer], sgu_b[layer],
                          seg0=sgu_seg0)
        r = _out_projection(attn, conv, sgu, w_out[layer].astype(BF16), xf, alpha=alpha)
        xf, xb = _post_layer_norm(r, post_ln_g[layer], post_ln_b[layer])
    return xf.reshape(B, S, D)
```

```python
import functools
import math

import jax
import jax.numpy as jnp
from jax import lax
from jax.experimental import pallas as pl
from jax.experimental.pallas import tpu as pltpu

F32 = jnp.float32
BF16 = jnp.bfloat16

HEAD_DIM = 64
HEAD_W = 2 * HEAD_DIM
CONV_K = 31
CONV_HALO = 32
CHUNK = 128
SGU_GROUP_W = 128
ROPE_THETA = 10000.0
LN_EPS = 1e-5
RMS_EPS = 1e-5
MASK_VALUE = -1e30
Q_SCALE = HEAD_DIM ** -0.5 * math.log2(math.e)
SUBLANES = 8
VMEM_LIMIT = 56 * 1024 * 1024


def _cparams(*sem):
    return pltpu.CompilerParams(dimension_semantics=sem, vmem_limit_bytes=VMEM_LIMIT)


def _layer_norm_rows(x, g, b):
    mu = jnp.mean(x, axis=-1, keepdims=True)
    xc = x - mu
    var = jnp.mean(xc * xc, axis=-1, keepdims=True)
    return xc * lax.rsqrt(var + LN_EPS) * g + b


def _silu(x):
    return x * jax.nn.sigmoid(x)


def _gelu(x):
    return 0.5 * x * (1.0 + lax.erf(x * (1.0 / math.sqrt(2.0))))


def _rope_body(pos_ref, inv_ref, cos_ref, sin_ref):
    ang = pos_ref[...].astype(F32) * inv_ref[...]
    lane = lax.broadcasted_iota(jnp.int32, ang.shape, 1)
    first_half = (lane % HEAD_DIM) < (HEAD_DIM // 2)
    s = jnp.sin(ang)
    cos_ref[...] = jnp.cos(ang)
    sin_ref[...] = jnp.where(first_half, -s, s)


def _rope_tables(positions):
    S = positions.shape[-1]
    ts = min(S, 1024)
    inv = ROPE_THETA ** (-jnp.arange(0, HEAD_DIM, 2, dtype=F32) / HEAD_DIM)
    inv = jnp.tile(inv, HEAD_W // (HEAD_DIM // 2))[None, :]
    return pl.pallas_call(
        _rope_body,
        grid=(S // ts,),
        in_specs=[pl.BlockSpec((ts, 1), lambda i: (i, 0)),
                  pl.BlockSpec((1, HEAD_W), lambda i: (0, 0))],
        out_specs=[pl.BlockSpec((ts, HEAD_W), lambda i: (i, 0))] * 2,
        out_shape=[jax.ShapeDtypeStruct((S, HEAD_W), F32)] * 2,
        compiler_params=_cparams("parallel"),
        name="rope_tables",
    )(positions.reshape(S, 1), inv)


def _rope(t, cos, sin, scale):
    lane = lax.broadcasted_iota(jnp.int32, t.shape, 1)
    first_half = (lane % HEAD_DIM) < (HEAD_DIM // 2)
    rot = jnp.where(first_half,
                    pltpu.roll(t, HEAD_W - HEAD_DIM // 2, 1),
                    pltpu.roll(t, HEAD_DIM // 2, 1))
    return (t * cos + rot * sin) * scale


_POINTWISE = {"ident": lambda t: t, "silu": _silu, "sigmoid": jax.nn.sigmoid, "gelu": _gelu}
_ROPE_SCALE = {"rope_q": Q_SCALE, "rope_k": 1.0}


def _inproj_body(x_ref, w_ref, cos_ref, sin_ref, o_ref, *, kind, rm):
    tm, tn = o_ref.shape
    for r0 in range(0, tm, rm):
        rows = slice(r0, r0 + rm)
        a = jnp.dot(x_ref[rows, :], w_ref[...], preferred_element_type=F32)
        if kind in _ROPE_SCALE:
            cos, sin = cos_ref[rows, :], sin_ref[rows, :]
            for c0 in range(0, tn, HEAD_W):
                o_ref[rows, c0:c0 + HEAD_W] = _rope(a[:, c0:c0 + HEAD_W], cos, sin,
                                                    _ROPE_SCALE[kind]).astype(o_ref.dtype)
        else:
            o_ref[rows, :] = _POINTWISE[kind](a).astype(o_ref.dtype)


def _in_projection(xb, w, cos, sin, *, kind, col0, ncols, tm=1024, tn=512, rm=256):
    S, D = xb.shape
    tm = min(tm, S)
    j0 = col0 // tn
    return pl.pallas_call(
        functools.partial(_inproj_body, kind=kind, rm=rm),
        grid=(S // tm, ncols // tn),
        in_specs=[pl.BlockSpec((tm, D), lambda i, j: (i, 0)),
                  pl.BlockSpec((D, tn), lambda i, j: (0, j0 + j)),
                  pl.BlockSpec((tm, HEAD_W), lambda i, j: (i, 0)),
                  pl.BlockSpec((tm, HEAD_W), lambda i, j: (i, 0))],
        out_specs=pl.BlockSpec((tm, tn), lambda i, j: (i, j)),
        out_shape=jax.ShapeDtypeStruct((S, ncols), BF16),
        compiler_params=_cparams("parallel", "arbitrary"),
        name="in_projection_" + kind,
    )(xb, w, cos, sin)


def _attn_body(lam_ref, g_ref, q_ref, k_ref, v_ref, gate_ref, o_ref,
               qt_ref, vt_ref, st_ref, bm_ref, pt_ref, al_ref, m_ref, l_ref, acc_ref,
               *, tq, rc, pc, lambda_init):
    qi = pl.program_id(1)
    n_kb = vt_ref.shape[0]

    @pl.when(qi == 0)
    def _():
        for kb in range(n_kb):
            vt_ref[kb] = v_ref[kb * tq:(kb + 1) * tq, :].astype(F32).T.astype(vt_ref.dtype)

    qt = q_ref[...].astype(F32).T
    row = lax.broadcasted_iota(jnp.int32, qt.shape, 0)
    qt_ref[0] = jnp.where(row < HEAD_DIM, qt, 0.0).astype(qt_ref.dtype)
    qt_ref[1] = jnp.where(row >= HEAD_DIM, qt, 0.0).astype(qt_ref.dtype)
    m_ref[...] = jnp.full(m_ref.shape, MASK_VALUE, F32)
    l_ref[...] = jnp.zeros(l_ref.shape, F32)
    acc_ref[...] = jnp.zeros(acc_ref.shape, F32)

    halves = range(2)

    n_ch = tq // rc
    kc = 2 * rc

    def by_vreg(a):
        return a.reshape(a.shape[0] // SUBLANES, SUBLANES, a.shape[1])

    def score_chunk(b, slot, r, bm):
        k = k_ref[pl.ds(pl.multiple_of(b * tq + r * rc, rc), rc), :]
        out = []
        for c in halves:
            st = jnp.dot(k, qt_ref[c], preferred_element_type=F32)
            st_ref[slot, c, r * rc:(r + 1) * rc, :] = st
            cm = jnp.max(by_vreg(st), axis=0)
            out.append(cm if bm is None else jnp.maximum(bm[c], cm))
        return out

    def score_end(slot, bm):
        for c in halves:
            bm_ref[slot, c] = jnp.max(bm[c], axis=0, keepdims=True)

    def load_scores(slot, c, r0, n, masked):
        st = st_ref[slot, c, r0:r0 + n, :]
        if masked:
            k_id = lax.broadcasted_iota(jnp.int32, st.shape, 0) + r0
            q_id = lax.broadcasted_iota(jnp.int32, st.shape, 1)
            st = jnp.where(k_id <= q_id, st, MASK_VALUE)
        return st

    def prob_begin(slot, masked):
        if masked:
            bmax = []
            for c in halves:
                bm = jnp.max(by_vreg(load_scores(slot, c, 0, pc, True)), axis=0)
                for r0 in range(pc, tq, pc):
                    bm = jnp.maximum(bm, jnp.max(by_vreg(load_scores(slot, c, r0, pc, True)), axis=0))
                bmax.append(jnp.max(bm, axis=0, keepdims=True))
        else:
            bmax = [bm_ref[slot, c] for c in halves]
        m_prev = [m_ref[c] for c in halves]
        m_cur = [jnp.maximum(m_prev[c], bmax[c]) for c in halves]
        alpha = [jnp.exp2(m_prev[c] - m_cur[c]) for c in halves]
        return m_cur, alpha

    def prob_chunk(slot, r, masked, m_cur, psum):
        out = list(psum) if psum is not None else [None, None]
        for r0 in range(r * rc, (r + 1) * rc, pc):
            for c in halves:
                p = jnp.exp2(load_scores(slot, c, r0, pc, masked) - m_cur[c])
                pt_ref[slot, c, r0:r0 + pc, :] = p.astype(pt_ref.dtype)
                ps = jnp.sum(by_vreg(p), axis=0)
                out[c] = ps if out[c] is None else out[c] + ps
        return out

    def prob_end(slot, m_cur, alpha, psum):
        for c in halves:
            al_ref[slot, c] = alpha[c]
            l_ref[c] = alpha[c] * l_ref[c] + jnp.sum(psum[c], axis=0, keepdims=True)
            m_ref[c] = m_cur[c]

    def value_chunk(b, slot, j, part):
        cols = slice(j * kc, (j + 1) * kc)
        vt = vt_ref[b, :, cols]
        out = []
        for c in halves:
            pv = jnp.dot(vt, pt_ref[slot, c, cols, :], preferred_element_type=F32)
            out.append(pv if part is None else part[c] + pv)
        return out

    def value_end(slot, part):
        for c in halves:
            acc_ref[c] = al_ref[slot, c] * acc_ref[c] + part[c]

    def score(b, slot):
        bm = None
        for r in range(n_ch):
            bm = score_chunk(b, slot, r, bm)
        score_end(slot, bm)

    def prob(slot, masked):
        m_cur, alpha = prob_begin(slot, masked)
        psum = None
        for r in range(n_ch):
            psum = prob_chunk(slot, r, masked, m_cur, psum)
        prob_end(slot, m_cur, alpha, psum)

    def value(b, slot):
        part = None
        for j in range(tq // kc):
            part = value_chunk(b, slot, j, part)
        value_end(slot, part)

    score(0, 0)

    def trip(b, par):
        m_cur, alpha = prob_begin(par, False)
        bm = psum = part = None
        for r in range(n_ch):
            bm = score_chunk(b + 1, 1 - par, r, bm)
            psum = prob_chunk(par, r, False, m_cur, psum)
            if r % 2 == 1:
                part = value_chunk(b, par, r // 2, part)
        score_end(1 - par, bm)
        prob_end(par, m_cur, alpha, psum)
        value_end(par, part)

    def body(b, carry):
        lax.cond(b % 2 == 0, lambda: trip(b, 0), lambda: trip(b, 1))
        return carry

    lax.fori_loop(0, qi, body, 0)

    for par in range(2):
        @pl.when(qi % 2 == par)
        def _():
            prob(par, True)
            value(qi, par)

    lv = lam_ref[...]
    lam = (jnp.exp(jnp.sum(lv[0:1] * lv[1:2], axis=1, keepdims=True))
           - jnp.exp(jnp.sum(lv[2:3] * lv[3:4], axis=1, keepdims=True)) + lambda_init)
    ot = acc_ref[0] * (1.0 / l_ref[0]) - lam * (acc_ref[1] * (1.0 / l_ref[1]))
    ot = ot * lax.rsqrt(jnp.mean(ot * ot, axis=0, keepdims=True) + RMS_EPS)
    ot = ot * g_ref[...] * (1.0 - lambda_init)
    o_ref[...] = (ot.T * gate_ref[...].astype(F32)).astype(o_ref.dtype)


def _attention(q, k, v, gate, lam_vecs, subln_g, *, lambda_init, tq=512):
    S = q.shape[0]
    tq = min(tq, S)
    H = q.shape[1] // HEAD_W
    return pl.pallas_call(
        functools.partial(_attn_body, tq=tq, rc=min(128, tq // 2), pc=min(128, tq // 2),
                          lambda_init=lambda_init),
        grid=(H, S // tq),
        in_specs=[pl.BlockSpec((4, HEAD_DIM), lambda hh, i: (0, 0)),
                  pl.BlockSpec((HEAD_W, 1), lambda hh, i: (0, 0)),
                  pl.BlockSpec((tq, HEAD_W), lambda hh, i: (i, hh)),
                  pl.BlockSpec((S, HEAD_W), lambda hh, i: (0, hh)),
                  pl.BlockSpec((S, HEAD_W), lambda hh, i: (0, hh)),
                  pl.BlockSpec((tq, HEAD_W), lambda hh, i: (i, hh))],
        out_specs=pl.BlockSpec((tq, HEAD_W), lambda hh, i: (i, hh)),
        out_shape=jax.ShapeDtypeStruct((S, H * HEAD_W), BF16),
        scratch_shapes=[pltpu.VMEM((2, HEAD_W, tq), BF16),
                        pltpu.VMEM((S // tq, HEAD_W, tq), BF16),
                        pltpu.VMEM((2, 2, tq, tq), F32),
                        pltpu.VMEM((2, 2, 1, tq), F32),
                        pltpu.VMEM((2, 2, tq, tq), BF16),
                        pltpu.VMEM((2, 2, 1, tq), F32),
                        pltpu.VMEM((2, 1, tq), F32),
                        pltpu.VMEM((2, 1, tq), F32),
                        pltpu.VMEM((2, HEAD_W, tq), F32)],
        compiler_params=_cparams("arbitrary", "arbitrary"),
        name="diff_attention",
    )(lam_vecs, subln_g.reshape(HEAD_W, 1), q, k, v, gate)


def _conv_body(val_ref, glu_ref, pval_ref, pglu_ref, gate_ref, dww_ref, dwb_ref, lng_ref,
               lnb_ref, pww_ref, pwb_ref, o_ref, ext_ref, sh_ref, y_ref, *, rc, cc):
    i = pl.program_id(0)
    T, C = y_ref.shape
    L = T + CONV_HALO
    prev = pval_ref[...].astype(F32) * pglu_ref[...].astype(F32)
    ext_ref[0:CONV_HALO, :] = jnp.where(i > 0, prev, 0.0)
    ext_ref[CONV_HALO:L, :] = val_ref[...].astype(F32) * glu_ref[...].astype(F32)
    ext_ref[L:L + SUBLANES, :] = jnp.zeros((SUBLANES, C), F32)

    for s in range(1, SUBLANES):
        for c0 in range(0, C, cc):
            sh_ref[s - 1, :, c0:c0 + cc] = ext_ref[s:s + L, c0:c0 + cc]

    def window(off, r0, c0):
        s, whole = off % SUBLANES, off - off % SUBLANES
        rows = slice(r0 + whole, r0 + whole + rc)
        return ext_ref[rows, c0:c0 + cc] if s == 0 else sh_ref[s - 1, rows, c0:c0 + cc]

    base = CONV_HALO - (CONV_K - 1)
    for r0 in range(0, T, rc):
        for c0 in range(0, C, cc):
            acc = jnp.zeros((rc, cc), F32) + dwb_ref[:, c0:c0 + cc]
            for j in range(CONV_K):
                acc = acc + dww_ref[j:j + 1, c0:c0 + cc] * window(base + j, r0, c0)
            y_ref[r0:r0 + rc, c0:c0 + cc] = acc

    a = _silu(_layer_norm_rows(y_ref[...], lng_ref[...], lnb_ref[...]))
    z = jnp.dot(a.astype(BF16), pww_ref[...], preferred_element_type=F32) + pwb_ref[...]
    o_ref[...] = (z * gate_ref[...].astype(F32)).astype(o_ref.dtype)


def _conv_branch(val, glu, gate, dw_w, dw_b, ln_g, ln_b, pw_wb, pw_b, *, T=256):
    S, C = val.shape
    T = min(T, S)
    hb = T // CONV_HALO
    row = lambda a: a.reshape(1, C)
    prev_map = lambda i: (jnp.maximum(i * hb - 1, 0), 0)
    return pl.pallas_call(
        functools.partial(_conv_body, rc=64, cc=256),
        grid=(S // T,),
        in_specs=[pl.BlockSpec((T, C), lambda i: (i, 0)),
                  pl.BlockSpec((T, C), lambda i: (i, 0)),
                  pl.BlockSpec((CONV_HALO, C), prev_map),
                  pl.BlockSpec((CONV_HALO, C), prev_map),
                  pl.BlockSpec((T, C), lambda i: (i, 0)),
                  pl.BlockSpec((CONV_K, C), lambda i: (0, 0)),
                  pl.BlockSpec((1, C), lambda i: (0, 0)),
                  pl.BlockSpec((1, C), lambda i: (0, 0)),
                  pl.BlockSpec((1, C), lambda i: (0, 0)),
                  pl.BlockSpec((C, C), lambda i: (0, 0)),
                  pl.BlockSpec((1, C), lambda i: (0, 0))],
        out_specs=pl.BlockSpec((T, C), lambda i: (i, 0)),
        out_shape=jax.ShapeDtypeStruct((S, C), BF16),
        scratch_shapes=[pltpu.VMEM((T + CONV_HALO + SUBLANES, C), F32),
                        pltpu.VMEM((SUBLANES - 1, T + CONV_HALO, C), F32),
                        pltpu.VMEM((T, C), F32)],
        compiler_params=_cparams("parallel"),
        name="conv_branch",
    )(val, glu, val, glu, gate, dw_w, row(dw_b), row(ln_g), row(ln_b), pw_wb, row(pw_b))


def _sgu_body(u_ref, v_ref, gate_ref, lng_ref, lnb_ref, w_ref, b_ref, o_ref, vn_ref):
    T, W = vn_ref.shape
    vn_ref[...] = _layer_norm_rows(v_ref[...].astype(F32), lng_ref[...], lnb_ref[...]).astype(BF16)
    r_id = lax.broadcasted_iota(jnp.int32, (CHUNK, CHUNK), 0)
    c_id = lax.broadcasted_iota(jnp.int32, (CHUNK, CHUNK), 1)
    causal = c_id <= r_id
    for g in range(W // SGU_GROUP_W):
        cols = slice(g * SGU_GROUP_W, (g + 1) * SGU_GROUP_W)
        wg = jnp.where(causal, w_ref[g], 0.0).astype(BF16)
        bias = b_ref[g]
        for n in range(T // CHUNK):
            rows = slice(n * CHUNK, (n + 1) * CHUNK)
            mixed = jnp.dot(wg, vn_ref[rows, cols], preferred_element_type=F32) + bias
            o_ref[rows, cols] = (u_ref[rows, cols].astype(F32) * mixed
                                 * gate_ref[rows, cols].astype(F32)).astype(o_ref.dtype)


def _sgu_branch(uv, gate, ln_g, ln_b, w_s, b_s, *, T=512):
    S = uv.shape[0]
    G = w_s.shape[0]
    W = G * SGU_GROUP_W
    T = min(T, S)
    return pl.pallas_call(
        _sgu_body,
        grid=(S // T,),
        in_specs=[pl.BlockSpec((T, W), lambda i: (i, 0)),
                  pl.BlockSpec((T, W), lambda i: (i, 1)),
                  pl.BlockSpec((T, W), lambda i: (i, 0)),
                  pl.BlockSpec((1, W), lambda i: (0, 0)),
                  pl.BlockSpec((1, W), lambda i: (0, 0)),
                  pl.BlockSpec((G, CHUNK, CHUNK), lambda i: (0, 0, 0)),
                  pl.BlockSpec((G, CHUNK, 1), lambda i: (0, 0, 0))],
        out_specs=pl.BlockSpec((T, W), lambda i: (i, 0)),
        out_shape=jax.ShapeDtypeStruct((S, W), BF16),
        scratch_shapes=[pltpu.VMEM((T, W), BF16)],
        compiler_params=_cparams("parallel"),
        name="sgu_branch",
    )(uv, uv, gate, ln_g.reshape(1, W), ln_b.reshape(1, W), w_s, b_s.reshape(G, CHUNK, 1))


def _outproj_body(a_ref, c_ref, s_ref, wa_ref, wc_ref, ws_ref, x_ref, o_ref, *, alpha):
    y = jnp.dot(a_ref[...], wa_ref[...], preferred_element_type=F32)
    y = y + jnp.dot(c_ref[...], wc_ref[...], preferred_element_type=F32)
    y = y + jnp.dot(s_ref[...], ws_ref[...], preferred_element_type=F32)
    o_ref[...] = alpha * x_ref[...] + y


def _out_projection(attn, conv, sgu, w, x, *, alpha, tm=1024, tn=1024):
    S, D = x.shape
    tm = min(tm, S)
    wa, wc, ws = attn.shape[1], conv.shape[1], sgu.shape[1]
    return pl.pallas_call(
        functools.partial(_outproj_body, alpha=alpha),
        grid=(S // tm, D // tn),
        in_specs=[pl.BlockSpec((tm, wa), lambda i, j: (i, 0)),
                  pl.BlockSpec((tm, wc), lambda i, j: (i, 0)),
                  pl.BlockSpec((tm, ws), lambda i, j: (i, 0)),
                  pl.BlockSpec((wa, tn), lambda i, j: (0, j)),
                  pl.BlockSpec((wc, tn), lambda i, j: (wa // wc, j)),
                  pl.BlockSpec((ws, tn), lambda i, j: ((wa + wc) // ws, j)),
                  pl.BlockSpec((tm, tn), lambda i, j: (i, j))],
        out_specs=pl.BlockSpec((tm, tn), lambda i, j: (i, j)),
        out_shape=jax.ShapeDtypeStruct((S, D), F32),
        compiler_params=_cparams("parallel", "arbitrary"),
        name="out_projection",
    )(attn, conv, sgu, w, w, w, x)


def _postln_body(r_ref, g_ref, b_ref, o_ref, ob_ref):
    y = _layer_norm_rows(r_ref[...], g_ref[...], b_ref[...])
    o_ref[...] = y
    ob_ref[...] = y.astype(ob_ref.dtype)


def _post_layer_norm(r, g, b, *, tr=256):
    S, D = r.shape
    tr = min(tr, S)
    return pl.pallas_call(
        _postln_body,
        grid=(S // tr,),
        in_specs=[pl.BlockSpec((tr, D), lambda i: (i, 0)),
                  pl.BlockSpec((1, D), lambda i: (0, 0)),
                  pl.BlockSpec((1, D), lambda i: (0, 0))],
        out_specs=[pl.BlockSpec((tr, D), lambda i: (i, 0))] * 2,
        out_shape=[jax.ShapeDtypeStruct((S, D), F32), jax.ShapeDtypeStruct((S, D), BF16)],
        compiler_params=_cparams("parallel"),
        name="post_layer_norm",
    )(r, g.reshape(1, D), b.reshape(1, D))


def kernel(x, positions, w_in, attn_lambda, attn_subln_g, conv_dw_w, conv_dw_b, conv_ln_g, conv_ln_b, conv_pw_w, conv_pw_b, sgu_ln_g, sgu_ln_b, sgu_w, sgu_b, w_out, post_ln_g, post_ln_b):
    B, S, D = x.shape
    depth = w_in.shape[0]
    assert B == 1 and D % 4 == 0
    u = D // 4
    alpha = (2 * depth) ** 0.25
    segments = (("q", 0, 2, "rope_q"), ("k", 2, 2, "rope_k"), ("v", 4, 2, "ident"),
                ("a_gate", 6, 2, "silu"), ("c_val", 8, 1, "ident"), ("c_glu", 9, 1, "sigmoid"),
                ("c_gate", 10, 1, "silu"), ("s_uv", 11, 2, "gelu"), ("s_gate", 13, 1, "silu"))

    cos, sin = _rope_tables(positions)
    xf = x.reshape(S, D)
    xb = xf.astype(BF16)
    for layer in range(depth):
        lambda_init = 0.8 - 0.6 * math.exp(-0.3 * layer)
        wb = w_in[layer].astype(BF16)
        h = {name: _in_projection(xb, wb, cos, sin, kind=kind, col0=first * u, ncols=n * u)
             for name, first, n, kind in segments}
        attn = _attention(h["q"], h["k"], h["v"], h["a_gate"], attn_lambda[layer],
                          attn_subln_g[layer], lambda_init=lambda_init)
        conv = _conv_branch(h["c_val"], h["c_glu"], h["c_gate"], conv_dw_w[layer],
                            conv_dw_b[layer], conv_ln_g[layer], conv_ln_b[layer],
                            conv_pw_w[layer].astype(BF16), conv_pw_b[layer])
        sgu = _sgu_branch(h["s_uv"], h["s_gate"], sgu_ln_g[layer], sgu_ln_b[layer],
                          sgu_w[layer], sgu_b[layer])
        r = _out_projection(attn, conv, sgu, w_out[layer].astype(BF16), xf, alpha=alpha)
        xf, xb = _post_layer_norm(r, post_ln_g[layer], post_ln_b[layer])
    return xf.reshape(B, S, D)
```

```python
import functools
import math

import jax
import jax.numpy as jnp
from jax import lax
from jax.experimental import pallas as pl
from jax.experimental.pallas import tpu as pltpu

F32 = jnp.float32
BF16 = jnp.bfloat16

HEAD_DIM = 64
HEAD_W = 2 * HEAD_DIM
CONV_K = 31
CONV_HALO = 32
CHUNK = 128
SGU_GROUP_W = 128
ROPE_THETA = 10000.0
LN_EPS = 1e-5
RMS_EPS = 1e-5
MASK_VALUE = -1e30
Q_SCALE = HEAD_DIM ** -0.5 * math.log2(math.e)
SUBLANES = 8
VMEM_LIMIT = 56 * 1024 * 1024


def _cparams(*sem):
    return pltpu.CompilerParams(dimension_semantics=sem, vmem_limit_bytes=VMEM_LIMIT)


def _layer_norm_rows(x, g, b):
    mu = jnp.mean(x, axis=-1, keepdims=True)
    xc = x - mu
    var = jnp.mean(xc * xc, axis=-1, keepdims=True)
    return xc * lax.rsqrt(var + LN_EPS) * g + b


def _silu(x):
    return x * jax.nn.sigmoid(x)


def _gelu(x):
    return 0.5 * x * (1.0 + lax.erf(x * (1.0 / math.sqrt(2.0))))


def _rope_body(pos_ref, inv_ref, cos_ref, sin_ref):
    ang = pos_ref[...].astype(F32) * inv_ref[...]
    lane = lax.broadcasted_iota(jnp.int32, ang.shape, 1)
    first_half = (lane % HEAD_DIM) < (HEAD_DIM // 2)
    s = jnp.sin(ang)
    cos_ref[...] = jnp.cos(ang)
    sin_ref[...] = jnp.where(first_half, -s, s)


def _rope_tables(positions):
    S = positions.shape[-1]
    ts = min(S, 1024)
    inv = ROPE_THETA ** (-jnp.arange(0, HEAD_DIM, 2, dtype=F32) / HEAD_DIM)
    inv = jnp.tile(inv, HEAD_W // (HEAD_DIM // 2))[None, :]
    return pl.pallas_call(
        _rope_body,
        grid=(S // ts,),
        in_specs=[pl.BlockSpec((ts, 1), lambda i: (i, 0)),
                  pl.BlockSpec((1, HEAD_W), lambda i: (0, 0))],
        out_specs=[pl.BlockSpec((ts, HEAD_W), lambda i: (i, 0))] * 2,
        out_shape=[jax.ShapeDtypeStruct((S, HEAD_W), F32)] * 2,
        compiler_params=_cparams("parallel"),
        name="rope_tables",
    )(positions.reshape(S, 1), inv)


def _rope(t, cos, sin, scale):
    lane = lax.broadcasted_iota(jnp.int32, t.shape, 1)
    first_half = (lane % HEAD_DIM) < (HEAD_DIM // 2)
    rot = jnp.where(first_half,
                    pltpu.roll(t, HEAD_W - HEAD_DIM // 2, 1),
                    pltpu.roll(t, HEAD_DIM // 2, 1))
    return (t * cos + rot * sin) * scale


_POINTWISE = {"ident": lambda t: t, "silu": _silu, "sigmoid": jax.nn.sigmoid, "gelu": _gelu}
_ROPE_SCALE = {"rope_q": Q_SCALE, "rope_k": 1.0}


def _inproj_body(x_ref, w_ref, cos_ref, sin_ref, o_ref, *, kind, rm):
    tm, tn = o_ref.shape
    for r0 in range(0, tm, rm):
        rows = slice(r0, r0 + rm)
        a = jnp.dot(x_ref[rows, :], w_ref[...], preferred_element_type=F32)
        if kind in _ROPE_SCALE:
            cos, sin = cos_ref[rows, :], sin_ref[rows, :]
            for c0 in range(0, tn, HEAD_W):
                o_ref[rows, c0:c0 + HEAD_W] = _rope(a[:, c0:c0 + HEAD_W], cos, sin,
                                                    _ROPE_SCALE[kind]).astype(o_ref.dtype)
        else:
            o_ref[rows, :] = _POINTWISE[kind](a).astype(o_ref.dtype)


def _cast_body(x_ref, o_ref):
    o_ref[...] = x_ref[...].astype(o_ref.dtype)


def _to_bf16(a, *, block_bytes=8 * 1024 * 1024):
    L, R, C = a.shape
    tr = R
    while tr * C * 4 > block_bytes and tr % 16 == 0:
        tr //= 2
    return pl.pallas_call(
        _cast_body,
        grid=(L, R // tr),
        in_specs=[pl.BlockSpec((None, tr, C), lambda l, i: (l, i, 0))],
        out_specs=pl.BlockSpec((None, tr, C), lambda l, i: (l, i, 0)),
        out_shape=jax.ShapeDtypeStruct(a.shape, BF16),
        compiler_params=_cparams("parallel", "parallel"),
        name="to_bf16",
    )(a)


def _in_projection(xb, w, layer, cos, sin, *, kind, col0, ncols, tm=1024, tn=512, rm=256):
    S, D = xb.shape
    tm = min(tm, S)
    j0 = col0 // tn
    return pl.pallas_call(
        functools.partial(_inproj_body, kind=kind, rm=rm),
        grid=(S // tm, ncols // tn),
        in_specs=[pl.BlockSpec((tm, D), lambda i, j: (i, 0)),
                  pl.BlockSpec((None, D, tn), lambda i, j: (layer, 0, j0 + j)),
                  pl.BlockSpec((tm, HEAD_W), lambda i, j: (i, 0)),
                  pl.BlockSpec((tm, HEAD_W), lambda i, j: (i, 0))],
        out_specs=pl.BlockSpec((tm, tn), lambda i, j: (i, j)),
        out_shape=jax.ShapeDtypeStruct((S, ncols), BF16),
        compiler_params=_cparams("parallel", "arbitrary"),
        name="in_projection_" + kind,
    )(xb, w, cos, sin)


def _attn_body(lam_ref, g_ref, q_ref, k_ref, v_ref, gate_ref, o_ref,
               qt_ref, vt_ref, st_ref, bm_ref, pt_ref, al_ref, m_ref, l_ref, acc_ref,
               *, tq, sc, rc, pc, lambda_init):
    qi = pl.program_id(1)
    n_kb = vt_ref.shape[0]

    @pl.when(qi == 0)
    def _():
        for kb in range(n_kb):
            vt_ref[kb] = v_ref[kb * tq:(kb + 1) * tq, :].astype(F32).T.astype(vt_ref.dtype)

    qt = q_ref[...].astype(F32).T
    row = lax.broadcasted_iota(jnp.int32, qt.shape, 0)
    qt_ref[0] = jnp.where(row < HEAD_DIM, qt, 0.0).astype(qt_ref.dtype)
    qt_ref[1] = jnp.where(row >= HEAD_DIM, qt, 0.0).astype(qt_ref.dtype)
    m_ref[...] = jnp.full(m_ref.shape, MASK_VALUE, F32)
    l_ref[...] = jnp.zeros(l_ref.shape, F32)
    acc_ref[...] = jnp.zeros(acc_ref.shape, F32)

    halves = range(2)

    n_ch = tq // rc
    kc = 2 * rc
    assert tq == 2 * kc and tq % sc == 0

    def by_vreg(a):
        return a.reshape(a.shape[0] // SUBLANES, SUBLANES, a.shape[1])

    def score_chunk(b, slot, r, bm):
        k = k_ref[pl.ds(pl.multiple_of(b * tq + r * sc, sc), sc), :]
        out = []
        for c in halves:
            st = jnp.dot(k, qt_ref[c], preferred_element_type=F32)
            st_ref[slot, c, r * sc:(r + 1) * sc, :] = st
            cm = jnp.max(by_vreg(st), axis=0)
            out.append(cm if bm is None else jnp.maximum(bm[c], cm))
        return out

    def score_end(slot, bm):
        for c in halves:
            bm_ref[slot, c] = jnp.max(bm[c], axis=0, keepdims=True)

    def load_scores(slot, c, r0, n, masked):
        st = st_ref[slot, c, r0:r0 + n, :]
        if masked:
            k_id = lax.broadcasted_iota(jnp.int32, st.shape, 0) + r0
            q_id = lax.broadcasted_iota(jnp.int32, st.shape, 1)
            st = jnp.where(k_id <= q_id, st, MASK_VALUE)
        return st

    def prob_begin(slot, masked):
        if masked:
            bmax = []
            for c in halves:
                bm = jnp.max(by_vreg(load_scores(slot, c, 0, pc, True)), axis=0)
                for r0 in range(pc, tq, pc):
                    bm = jnp.maximum(bm, jnp.max(by_vreg(load_scores(slot, c, r0, pc, True)), axis=0))
                bmax.append(jnp.max(bm, axis=0, keepdims=True))
        else:
            bmax = [bm_ref[slot, c] for c in halves]
        m_prev = [m_ref[c] for c in halves]
        m_cur = [jnp.maximum(m_prev[c], bmax[c]) for c in halves]
        alpha = [jnp.exp2(m_prev[c] - m_cur[c]) for c in halves]
        return m_cur, alpha

    def prob_chunk(slot, r, masked, m_cur, psum):
        out = list(psum) if psum is not None else [None, None]
        for r0 in range(r * rc, (r + 1) * rc, pc):
            for c in halves:
                p = jnp.exp2(load_scores(slot, c, r0, pc, masked) - m_cur[c])
                pt_ref[slot, c, r0:r0 + pc, :] = p.astype(pt_ref.dtype)
                ps = jnp.sum(by_vreg(p), axis=0)
                out[c] = ps if out[c] is None else out[c] + ps
        return out

    def prob_end(slot, m_cur, alpha, psum):
        for c in halves:
            al_ref[slot, c] = alpha[c]
            l_ref[c] = alpha[c] * l_ref[c] + jnp.sum(psum[c], axis=0, keepdims=True)
            m_ref[c] = m_cur[c]

    def value_chunk(b, slot, j, part):
        cols = slice(j * kc, (j + 1) * kc)
        vt = vt_ref[b, :, cols]
        out = []
        for c in halves:
            pv = jnp.dot(vt, pt_ref[slot, c, cols, :], preferred_element_type=F32)
            out.append(pv if part is None else part[c] + pv)
        return out

    def value_end(slot, part):
        for c in halves:
            acc_ref[c] = al_ref[slot, c] * acc_ref[c] + part[c]

    def score(b, slot):
        bm = None
        for r in range(tq // sc):
            bm = score_chunk(b, slot, r, bm)
        score_end(slot, bm)

    def prob(slot, masked):
        m_cur, alpha = prob_begin(slot, masked)
        psum = None
        for r in range(n_ch):
            psum = prob_chunk(slot, r, masked, m_cur, psum)
        prob_end(slot, m_cur, alpha, psum)

    def value(b, slot):
        part = None
        for j in range(tq // kc):
            part = value_chunk(b, slot, j, part)
        value_end(slot, part)

    score(0, 0)

    def trip(b, par):
        m_cur, alpha = prob_begin(par, False)
        bm = psum = part = None
        for r in range(n_ch):
            if (r * rc) % sc == 0:
                bm = score_chunk(b + 1, 1 - par, (r * rc) // sc, bm)
            psum = prob_chunk(par, r, False, m_cur, psum)
            if r % 2 == 1:
                part = value_chunk(b, par, r // 2, part)
        score_end(1 - par, bm)
        prob_end(par, m_cur, alpha, psum)
        value_end(par, part)

    def body(b, carry):
        lax.cond(b % 2 == 0, lambda: trip(b, 0), lambda: trip(b, 1))
        return carry

    lax.fori_loop(0, qi, body, 0)

    for par in range(2):
        @pl.when(qi % 2 == par)
        def _():
            prob(par, True)
            value(qi, par)

    lv = lam_ref[...]
    lam = (jnp.exp(jnp.sum(lv[0:1] * lv[1:2], axis=1, keepdims=True))
           - jnp.exp(jnp.sum(lv[2:3] * lv[3:4], axis=1, keepdims=True)) + lambda_init)
    ot = acc_ref[0] * (1.0 / l_ref[0]) - lam * (acc_ref[1] * (1.0 / l_ref[1]))
    ot = ot * lax.rsqrt(jnp.mean(ot * ot, axis=0, keepdims=True) + RMS_EPS)
    ot = ot * g_ref[...] * (1.0 - lambda_init)
    o_ref[...] = (ot.T * gate_ref[...].astype(F32)).astype(o_ref.dtype)


def _attention(q, k, v, gate, lam_vecs, subln_g, *, lambda_init, tq=512):
    S = q.shape[0]
    tq = min(tq, S)
    H = q.shape[1] // HEAD_W
    return pl.pallas_call(
        functools.partial(_attn_body, tq=tq, sc=tq, rc=min(128, tq // 2), pc=min(128, tq // 2),
                          lambda_init=lambda_init),
        grid=(H, S // tq),
        in_specs=[pl.BlockSpec((4, HEAD_DIM), lambda hh, i: (0, 0)),
                  pl.BlockSpec((HEAD_W, 1), lambda hh, i: (0, 0)),
                  pl.BlockSpec((tq, HEAD_W), lambda hh, i: (i, hh)),
                  pl.BlockSpec((S, HEAD_W), lambda hh, i: (0, hh)),
                  pl.BlockSpec((S, HEAD_W), lambda hh, i: (0, hh)),
                  pl.BlockSpec((tq, HEAD_W), lambda hh, i: (i, hh))],
        out_specs=pl.BlockSpec((tq, HEAD_W), lambda hh, i: (i, hh)),
        out_shape=jax.ShapeDtypeStruct((S, H * HEAD_W), BF16),
        scratch_shapes=[pltpu.VMEM((2, HEAD_W, tq), BF16),
                        pltpu.VMEM((S // tq, HEAD_W, tq), BF16),
                        pltpu.VMEM((2, 2, tq, tq), F32),
                        pltpu.VMEM((2, 2, 1, tq), F32),
                        pltpu.VMEM((2, 2, tq, tq), BF16),
                        pltpu.VMEM((2, 2, 1, tq), F32),
                        pltpu.VMEM((2, 1, tq), F32),
                        pltpu.VMEM((2, 1, tq), F32),
                        pltpu.VMEM((2, HEAD_W, tq), F32)],
        compiler_params=_cparams("arbitrary", "arbitrary"),
        name="diff_attention",
    )(lam_vecs, subln_g.reshape(HEAD_W, 1), q, k, v, gate)


def _conv_body(val_ref, glu_ref, pval_ref, pglu_ref, gate_ref, dww_ref, dwb_ref, lng_ref,
               lnb_ref, pww_ref, pwb_ref, o_ref, ext_ref, sh_ref, y_ref, *, rc, cc):
    i = pl.program_id(0)
    T, C = y_ref.shape
    L = T + CONV_HALO
    prev = pval_ref[...].astype(F32) * pglu_ref[...].astype(F32)
    ext_ref[0:CONV_HALO, :] = jnp.where(i > 0, prev, 0.0)
    ext_ref[CONV_HALO:L, :] = val_ref[...].astype(F32) * glu_ref[...].astype(F32)
    ext_ref[L:L + SUBLANES, :] = jnp.zeros((SUBLANES, C), F32)

    for s in range(1, SUBLANES):
        for c0 in range(0, C, cc):
            sh_ref[s - 1, :, c0:c0 + cc] = ext_ref[s:s + L, c0:c0 + cc]

    def window(off, r0, c0):
        s, whole = off % SUBLANES, off - off % SUBLANES
        rows = slice(r0 + whole, r0 + whole + rc)
        return ext_ref[rows, c0:c0 + cc] if s == 0 else sh_ref[s - 1, rows, c0:c0 + cc]

    base = CONV_HALO - (CONV_K - 1)
    for r0 in range(0, T, rc):
        for c0 in range(0, C, cc):
            acc = jnp.zeros((rc, cc), F32) + dwb_ref[:, c0:c0 + cc]
            for j in range(CONV_K):
                acc = acc + dww_ref[j:j + 1, c0:c0 + cc] * window(base + j, r0, c0)
            y_ref[r0:r0 + rc, c0:c0 + cc] = acc

    a = _silu(_layer_norm_rows(y_ref[...], lng_ref[...], lnb_ref[...]))
    z = jnp.dot(a.astype(BF16), pww_ref[...], preferred_element_type=F32) + pwb_ref[...]
    o_ref[...] = (z * gate_ref[...].astype(F32)).astype(o_ref.dtype)


def _conv_branch(val, glu, gate, dw_w, dw_b, ln_g, ln_b, pw_wb, layer, pw_b, *, T=256):
    S, C = val.shape
    T = min(T, S)
    hb = T // CONV_HALO
    row = lambda a: a.reshape(1, C)
    prev_map = lambda i: (jnp.maximum(i * hb - 1, 0), 0)
    return pl.pallas_call(
        functools.partial(_conv_body, rc=64, cc=256),
        grid=(S // T,),
        in_specs=[pl.BlockSpec((T, C), lambda i: (i, 0)),
                  pl.BlockSpec((T, C), lambda i: (i, 0)),
                  pl.BlockSpec((CONV_HALO, C), prev_map),
                  pl.BlockSpec((CONV_HALO, C), prev_map),
                  pl.BlockSpec((T, C), lambda i: (i, 0)),
                  pl.BlockSpec((CONV_K, C), lambda i: (0, 0)),
                  pl.BlockSpec((1, C), lambda i: (0, 0)),
                  pl.BlockSpec((1, C), lambda i: (0, 0)),
                  pl.BlockSpec((1, C), lambda i: (0, 0)),
                  pl.BlockSpec((None, C, C), lambda i: (layer, 0, 0)),
                  pl.BlockSpec((1, C), lambda i: (0, 0))],
        out_specs=pl.BlockSpec((T, C), lambda i: (i, 0)),
        out_shape=jax.ShapeDtypeStruct((S, C), BF16),
        scratch_shapes=[pltpu.VMEM((T + CONV_HALO + SUBLANES, C), F32),
                        pltpu.VMEM((SUBLANES - 1, T + CONV_HALO, C), F32),
                        pltpu.VMEM((T, C), F32)],
        compiler_params=_cparams("parallel"),
        name="conv_branch",
    )(val, glu, val, glu, gate, dw_w, row(dw_b), row(ln_g), row(ln_b), pw_wb, row(pw_b))


def _sgu_body(u_ref, v_ref, gate_ref, lng_ref, lnb_ref, w_ref, b_ref, o_ref, vn_ref):
    T, W = vn_ref.shape
    vn_ref[...] = _layer_norm_rows(v_ref[...].astype(F32), lng_ref[...], lnb_ref[...]).astype(BF16)
    r_id = lax.broadcasted_iota(jnp.int32, (CHUNK, CHUNK), 0)
    c_id = lax.broadcasted_iota(jnp.int32, (CHUNK, CHUNK), 1)
    causal = c_id <= r_id
    for g in range(W // SGU_GROUP_W):
        cols = slice(g * SGU_GROUP_W, (g + 1) * SGU_GROUP_W)
        wg = jnp.where(causal, w_ref[g], 0.0).astype(BF16)
        bias = b_ref[g]
        for n in range(T // CHUNK):
            rows = slice(n * CHUNK, (n + 1) * CHUNK)
            mixed = jnp.dot(wg, vn_ref[rows, cols], preferred_element_type=F32) + bias
            o_ref[rows, cols] = (u_ref[rows, cols].astype(F32) * mixed
                                 * gate_ref[rows, cols].astype(F32)).astype(o_ref.dtype)


def _sgu_branch(uv, gate, ln_g, ln_b, w_s, b_s, *, T=512):
    S = uv.shape[0]
    G = w_s.shape[0]
    W = G * SGU_GROUP_W
    T = min(T, S)
    return pl.pallas_call(
        _sgu_body,
        grid=(S // T,),
        in_specs=[pl.BlockSpec((T, W), lambda i: (i, 0)),
                  pl.BlockSpec((T, W), lambda i: (i, 1)),
                  pl.BlockSpec((T, W), lambda i: (i, 0)),
                  pl.BlockSpec((1, W), lambda i: (0, 0)),
                  pl.BlockSpec((1, W), lambda i: (0, 0)),
                  pl.BlockSpec((G, CHUNK, CHUNK), lambda i: (0, 0, 0)),
                  pl.BlockSpec((G, CHUNK, 1), lambda i: (0, 0, 0))],
        out_specs=pl.BlockSpec((T, W), lambda i: (i, 0)),
        out_shape=jax.ShapeDtypeStruct((S, W), BF16),
        scratch_shapes=[pltpu.VMEM((T, W), BF16)],
        compiler_params=_cparams("parallel"),
        name="sgu_branch",
    )(uv, uv, gate, ln_g.reshape(1, W), ln_b.reshape(1, W), w_s, b_s.reshape(G, CHUNK, 1))


def _outproj_body(a_ref, c_ref, s_ref, wa_ref, wc_ref, ws_ref, x_ref, o_ref, *, alpha):
    y = jnp.dot(a_ref[...], wa_ref[...], preferred_element_type=F32)
    y = y + jnp.dot(c_ref[...], wc_ref[...], preferred_element_type=F32)
    y = y + jnp.dot(s_ref[...], ws_ref[...], preferred_element_type=F32)
    o_ref[...] = alpha * x_ref[...] + y


def _out_projection(attn, conv, sgu, w, layer, x, *, alpha, tm=1024, tn=1024):
    S, D = x.shape
    tm = min(tm, S)
    wa, wc, ws = attn.shape[1], conv.shape[1], sgu.shape[1]
    return pl.pallas_call(
        functools.partial(_outproj_body, alpha=alpha),
        grid=(S // tm, D // tn),
        in_specs=[pl.BlockSpec((tm, wa), lambda i, j: (i, 0)),
                  pl.BlockSpec((tm, wc), lambda i, j: (i, 0)),
                  pl.BlockSpec((tm, ws), lambda i, j: (i, 0)),
                  pl.BlockSpec((None, wa, tn), lambda i, j: (layer, 0, j)),
                  pl.BlockSpec((None, wc, tn), lambda i, j: (layer, wa // wc, j)),
                  pl.BlockSpec((None, ws, tn), lambda i, j: (layer, (wa + wc) // ws, j)),
                  pl.BlockSpec((tm, tn), lambda i, j: (i, j))],
        out_specs=pl.BlockSpec((tm, tn), lambda i, j: (i, j)),
        out_shape=jax.ShapeDtypeStruct((S, D), F32),
        compiler_params=_cparams("parallel", "arbitrary"),
        name="out_projection",
    )(attn, conv, sgu, w, w, w, x)


def _postln_body(r_ref, g_ref, b_ref, o_ref, ob_ref):
    y = _layer_norm_rows(r_ref[...], g_ref[...], b_ref[...])
    o_ref[...] = y
    ob_ref[...] = y.astype(ob_ref.dtype)


def _post_layer_norm(r, g, b, *, tr=256):
    S, D = r.shape
    tr = min(tr, S)
    return pl.pallas_call(
        _postln_body,
        grid=(S // tr,),
        in_specs=[pl.BlockSpec((tr, D), lambda i: (i, 0)),
                  pl.BlockSpec((1, D), lambda i: (0, 0)),
                  pl.BlockSpec((1, D), lambda i: (0, 0))],
        out_specs=[pl.BlockSpec((tr, D), lambda i: (i, 0))] * 2,
        out_shape=[jax.ShapeDtypeStruct((S, D), F32), jax.ShapeDtypeStruct((S, D), BF16)],
        compiler_params=_cparams("parallel"),
        name="post_layer_norm",
    )(r, g.reshape(1, D), b.reshape(1, D))


def kernel(x, positions, w_in, attn_lambda, attn_subln_g, conv_dw_w, conv_dw_b, conv_ln_g, conv_ln_b, conv_pw_w, conv_pw_b, sgu_ln_g, sgu_ln_b, sgu_w, sgu_b, w_out, post_ln_g, post_ln_b):
    B, S, D = x.shape
    depth = w_in.shape[0]
    assert B == 1 and D % 4 == 0
    u = D // 4
    alpha = (2 * depth) ** 0.25
    segments = (("q", 0, 2, "rope_q"), ("k", 2, 2, "rope_k"), ("v", 4, 2, "ident"),
                ("a_gate", 6, 2, "silu"), ("c_val", 8, 1, "ident"), ("c_glu", 9, 1, "sigmoid"),
                ("c_gate", 10, 1, "silu"), ("s_uv", 11, 2, "gelu"), ("s_gate", 13, 1, "silu"))

    cos, sin = _rope_tables(positions)
    xf = x.reshape(S, D)
    xb = _to_bf16(x)[0]
    w_in_b, w_out_b, pw_w_b = _to_bf16(w_in), _to_bf16(w_out), _to_bf16(conv_pw_w)
    for layer in range(depth):
        lambda_init = 0.8 - 0.6 * math.exp(-0.3 * layer)
        h = {name: _in_projection(xb, w_in_b, layer, cos, sin, kind=kind, col0=first * u,
                                  ncols=n * u)
             for name, first, n, kind in segments}
        attn = _attention(h["q"], h["k"], h["v"], h["a_gate"], attn_lambda[layer],
                          attn_subln_g[layer], lambda_init=lambda_init)
        conv = _conv_branch(h["c_val"], h["c_glu"], h["c_gate"], conv_dw_w[layer],
                            conv_dw_b[layer], conv_ln_g[layer], conv_ln_b[layer],
                            pw_w_b, layer, conv_pw_b[layer])
        sgu = _sgu_branch(h["s_uv"], h["s_gate"], sgu_ln_g[layer], sgu_ln_b[layer],
                          sgu_w[layer], sgu_b[layer])
        r = _out_projection(attn, conv, sgu, w_out_b, layer, xf, alpha=alpha)
        xf, xb = _post_layer_norm(r, post_ln_g[layer], post_ln_b[layer])
    return xf.reshape(B, S, D)
```

```python
import functools
import math

import jax
import jax.numpy as jnp
from jax import lax
from jax.experimental import pallas as pl
from jax.experimental.pallas import tpu as pltpu

F32 = jnp.float32
BF16 = jnp.bfloat16

HEAD_DIM = 64
HEAD_W = 2 * HEAD_DIM
CONV_K = 31
CONV_HALO = 32
CHUNK = 128
SGU_GROUP_W = 128
ROPE_THETA = 10000.0
LN_EPS = 1e-5
RMS_EPS = 1e-5
MASK_VALUE = -1e30
Q_SCALE = HEAD_DIM ** -0.5 * math.log2(math.e)
SUBLANES = 8
VMEM_LIMIT = 56 * 1024 * 1024


def _cparams(*sem):
    return pltpu.CompilerParams(dimension_semantics=sem, vmem_limit_bytes=VMEM_LIMIT)


def _layer_norm_rows(x, g, b):
    mu = jnp.mean(x, axis=-1, keepdims=True)
    xc = x - mu
    var = jnp.mean(xc * xc, axis=-1, keepdims=True)
    return xc * lax.rsqrt(var + LN_EPS) * g + b


def _silu(x):
    return x * jax.nn.sigmoid(x)


def _gelu(x):
    return 0.5 * x * (1.0 + lax.erf(x * (1.0 / math.sqrt(2.0))))


def _rope_body(pos_ref, inv_ref, cos_ref, sin_ref):
    ang = pos_ref[...].astype(F32) * inv_ref[...]
    lane = lax.broadcasted_iota(jnp.int32, ang.shape, 1)
    first_half = (lane % HEAD_DIM) < (HEAD_DIM // 2)
    s = jnp.sin(ang)
    cos_ref[...] = jnp.cos(ang)
    sin_ref[...] = jnp.where(first_half, -s, s)


def _rope_tables(positions):
    S = positions.shape[-1]
    ts = min(S, 1024)
    inv = ROPE_THETA ** (-jnp.arange(0, HEAD_DIM, 2, dtype=F32) / HEAD_DIM)
    inv = jnp.tile(inv, HEAD_W // (HEAD_DIM // 2))[None, :]
    return pl.pallas_call(
        _rope_body,
        grid=(S // ts,),
        in_specs=[pl.BlockSpec((ts, 1), lambda i: (i, 0)),
                  pl.BlockSpec((1, HEAD_W), lambda i: (0, 0))],
        out_specs=[pl.BlockSpec((ts, HEAD_W), lambda i: (i, 0))] * 2,
        out_shape=[jax.ShapeDtypeStruct((S, HEAD_W), F32)] * 2,
        compiler_params=_cparams("parallel"),
        name="rope_tables",
    )(positions.reshape(S, 1), inv)


def _rope(t, cos, sin, scale):
    lane = lax.broadcasted_iota(jnp.int32, t.shape, 1)
    first_half = (lane % HEAD_DIM) < (HEAD_DIM // 2)
    rot = jnp.where(first_half,
                    pltpu.roll(t, HEAD_W - HEAD_DIM // 2, 1),
                    pltpu.roll(t, HEAD_DIM // 2, 1))
    return (t * cos + rot * sin) * scale


_POINTWISE = {"ident": lambda t: t, "silu": _silu, "sigmoid": jax.nn.sigmoid, "gelu": _gelu}
_ROPE_SCALE = {"rope_q": Q_SCALE, "rope_k": 1.0}


def _inproj_body(x_ref, w_ref, cos_ref, sin_ref, o_ref, wb_ref, *, kind, rm, kchunk):
    tm, tn = o_ref.shape

    @pl.when(pl.program_id(1) == 0)
    def _():
        for k0 in range(0, w_ref.shape[0], kchunk):
            wb_ref[k0:k0 + kchunk, :] = w_ref[k0:k0 + kchunk, :].astype(wb_ref.dtype)

    for r0 in range(0, tm, rm):
        rows = slice(r0, r0 + rm)
        a = jnp.dot(x_ref[rows, :], wb_ref[...], preferred_element_type=F32)
        if kind in _ROPE_SCALE:
            cos, sin = cos_ref[rows, :], sin_ref[rows, :]
            for c0 in range(0, tn, HEAD_W):
                o_ref[rows, c0:c0 + HEAD_W] = _rope(a[:, c0:c0 + HEAD_W], cos, sin,
                                                    _ROPE_SCALE[kind]).astype(o_ref.dtype)
        else:
            o_ref[rows, :] = _POINTWISE[kind](a).astype(o_ref.dtype)


def _cast_body(x_ref, o_ref):
    o_ref[...] = x_ref[...].astype(o_ref.dtype)


def _to_bf16(a, *, block_bytes=8 * 1024 * 1024):
    L, R, C = a.shape
    tr = R
    while tr * C * 4 > block_bytes and tr % 16 == 0:
        tr //= 2
    return pl.pallas_call(
        _cast_body,
        grid=(L, R // tr),
        in_specs=[pl.BlockSpec((None, tr, C), lambda l, i: (l, i, 0))],
        out_specs=pl.BlockSpec((None, tr, C), lambda l, i: (l, i, 0)),
        out_shape=jax.ShapeDtypeStruct(a.shape, BF16),
        compiler_params=_cparams("parallel", "parallel"),
        name="to_bf16",
    )(a)


def _in_projection(xb, w, layer, cos, sin, *, kind, col0, ncols, tm=1024, tn=512, rm=256):
    S, D = xb.shape
    tm = min(tm, S)
    j0 = col0 // tn
    return pl.pallas_call(
        functools.partial(_inproj_body, kind=kind, rm=rm, kchunk=512),
        grid=(ncols // tn, S // tm),
        in_specs=[pl.BlockSpec((tm, D), lambda j, i: (i, 0)),
                  pl.BlockSpec((None, D, tn), lambda j, i: (layer, 0, j0 + j)),
                  pl.BlockSpec((tm, HEAD_W), lambda j, i: (i, 0)),
                  pl.BlockSpec((tm, HEAD_W), lambda j, i: (i, 0))],
        out_specs=pl.BlockSpec((tm, tn), lambda j, i: (i, j)),
        out_shape=jax.ShapeDtypeStruct((S, ncols), BF16),
        scratch_shapes=[pltpu.VMEM((D, tn), BF16)],
        compiler_params=_cparams("arbitrary", "arbitrary"),
        name="in_projection_" + kind,
    )(xb, w, cos, sin)


FIRST = 2


def _attn_body(lam_ref, g_ref, q_ref, qn_ref, k_ref, v_ref, gate_ref, o_ref,
               qt_ref, vt_ref, st_ref, bm_ref, pt_ref, al_ref, m_ref, l_ref, acc_ref,
               *, tq, sc, rc, lambda_init):
    qi = pl.program_id(1)
    halves = range(2)
    n_ch = tq // rc
    kc = 2 * rc
    assert tq % kc == 0 and tq % sc == 0

    def by_vreg(a):
        return a.reshape(a.shape[0] // SUBLANES, SUBLANES, a.shape[1])

    def score(b, slot):
        bm = [None, None]
        for r in range(tq // sc):
            k = k_ref[pl.ds(pl.multiple_of(b * tq + r * sc, sc), sc), :]
            for c in halves:
                st = jnp.dot(k, qt_ref[c], preferred_element_type=F32)
                st_ref[slot, c, r * sc:(r + 1) * sc, :] = st
                cm = jnp.max(by_vreg(st), axis=0)
                bm[c] = cm if bm[c] is None else jnp.maximum(bm[c], cm)
        return bm

    def score_end(slot, bm):
        for c in halves:
            bm_ref[slot, c] = jnp.max(bm[c], axis=0, keepdims=True)

    def load_scores(slot, c, r, masked):
        st = st_ref[slot, c, r * rc:(r + 1) * rc, :]
        if masked:
            k_id = lax.broadcasted_iota(jnp.int32, st.shape, 0) + r * rc
            q_id = lax.broadcasted_iota(jnp.int32, st.shape, 1)
            st = jnp.where(k_id <= q_id, st, MASK_VALUE)
        return st

    def prob_begin(slot, masked):
        if masked:
            bmax = []
            for c in halves:
                bm = jnp.max(by_vreg(load_scores(slot, c, 0, True)), axis=0)
                for r in range(1, n_ch):
                    bm = jnp.maximum(bm, jnp.max(by_vreg(load_scores(slot, c, r, True)), axis=0))
                bmax.append(jnp.max(bm, axis=0, keepdims=True))
        else:
            bmax = [bm_ref[slot, c] for c in halves]
        m_prev = [m_ref[c] for c in halves]
        m_cur = [jnp.maximum(m_prev[c], bmax[c]) for c in halves]
        alpha = [jnp.exp2(m_prev[c] - m_cur[c]) for c in halves]
        return m_cur, alpha

    def prob_chunk(slot, par, r, masked, m_cur, psum):
        for c in halves:
            p = jnp.exp2(load_scores(slot, c, r, masked) - m_cur[c])
            pt_ref[par, c, r * rc:(r + 1) * rc, :] = p.astype(pt_ref.dtype)
            ps = jnp.sum(by_vreg(p), axis=0)
            psum[c] = ps if psum[c] is None else psum[c] + ps

    def prob_end(par, m_cur, alpha, psum):
        for c in halves:
            al_ref[par, c] = alpha[c]
            l_ref[c] = alpha[c] * l_ref[c] + jnp.sum(psum[c], axis=0, keepdims=True)
            m_ref[c] = m_cur[c]

    def value_chunk(b, par, j, part):
        cols = slice(j * kc, (j + 1) * kc)
        vt = vt_ref[b, :, cols]
        for c in halves:
            pv = jnp.dot(vt, pt_ref[par, c, cols, :], preferred_element_type=F32)
            part[c] = pv if part[c] is None else part[c] + pv

    def value_end(par, part):
        for c in halves:
            acc_ref[c] = al_ref[par, c] * acc_ref[c] + part[c]

    def consume(b, slot, par, masked, next_slot=None):
        m_cur, alpha = prob_begin(slot, masked)
        psum, part = [None, None], [None, None]
        bm = score(b + 1, next_slot) if next_slot is not None else None
        for r in range(n_ch):
            prob_chunk(slot, par, r, masked, m_cur, psum)
            if (r + 1) * rc % kc == 0:
                value_chunk(b, par, (r + 1) * rc // kc - 1, part)
        if bm is not None:
            score_end(next_slot, bm)
        prob_end(par, m_cur, alpha, psum)
        value_end(par, part)

    def start_block(ref):
        qt = ref[...].astype(F32).T
        row = lax.broadcasted_iota(jnp.int32, qt.shape, 0)
        qt_ref[0] = jnp.where(row < HEAD_DIM, qt, 0.0).astype(qt_ref.dtype)
        qt_ref[1] = jnp.where(row >= HEAD_DIM, qt, 0.0).astype(qt_ref.dtype)
        score_end(FIRST, score(0, FIRST))

    @pl.when(qi == 0)
    def _():
        for kb in range(vt_ref.shape[0]):
            vt_ref[kb] = v_ref[kb * tq:(kb + 1) * tq, :].astype(F32).T.astype(vt_ref.dtype)
        start_block(q_ref)

    m_ref[...] = jnp.full(m_ref.shape, MASK_VALUE, F32)
    l_ref[...] = jnp.zeros(l_ref.shape, F32)
    acc_ref[...] = jnp.zeros(acc_ref.shape, F32)

    @pl.when(qi > 0)
    def _():
        consume(0, FIRST, 0, False, next_slot=1)

    def body(b, carry):
        lax.cond(b % 2 == 0,
                 lambda: consume(b, 0, 0, False, next_slot=1),
                 lambda: consume(b, 1, 1, False, next_slot=0))
        return carry

    lax.fori_loop(1, qi, body, 0)

    for cond, slot, par in ((qi == 0, FIRST, 0),
                            ((qi > 0) & (qi % 2 == 0), 0, 0),
                            (qi % 2 == 1, 1, 1)):
        @pl.when(cond)
        def _():
            if slot == FIRST:
                consume(qi, slot, par, True)
                start_block(qn_ref)
            else:
                start_block(qn_ref)
                consume(qi, slot, par, True)

    lv = lam_ref[...]
    lam = (jnp.exp(jnp.sum(lv[0:1] * lv[1:2], axis=1, keepdims=True))
           - jnp.exp(jnp.sum(lv[2:3] * lv[3:4], axis=1, keepdims=True)) + lambda_init)
    ot = acc_ref[0] * (1.0 / l_ref[0]) - lam * (acc_ref[1] * (1.0 / l_ref[1]))
    ot = ot * lax.rsqrt(jnp.mean(ot * ot, axis=0, keepdims=True) + RMS_EPS)
    ot = ot * g_ref[...] * (1.0 - lambda_init)
    o_ref[...] = (ot.T * gate_ref[...].astype(F32)).astype(o_ref.dtype)


def _attention(q, k, v, gate, lam_vecs, subln_g, *, lambda_init, tq=512):
    S = q.shape[0]
    tq = min(tq, S)
    n_q = S // tq
    H = q.shape[1] // HEAD_W
    return pl.pallas_call(
        functools.partial(_attn_body, tq=tq, sc=tq, rc=min(128, tq // 2),
                          lambda_init=lambda_init),
        grid=(H, n_q),
        in_specs=[pl.BlockSpec((4, HEAD_DIM), lambda hh, i: (0, 0)),
                  pl.BlockSpec((HEAD_W, 1), lambda hh, i: (0, 0)),
                  pl.BlockSpec((tq, HEAD_W), lambda hh, i: (i, hh)),
                  pl.BlockSpec((tq, HEAD_W),
                               lambda hh, i: (jnp.minimum(i + 1, n_q - 1), hh)),
                  pl.BlockSpec((S, HEAD_W), lambda hh, i: (0, hh)),
                  pl.BlockSpec((S, HEAD_W), lambda hh, i: (0, hh)),
                  pl.BlockSpec((tq, HEAD_W), lambda hh, i: (i, hh))],
        out_specs=pl.BlockSpec((tq, HEAD_W), lambda hh, i: (i, hh)),
        out_shape=jax.ShapeDtypeStruct((S, H * HEAD_W), BF16),
        scratch_shapes=[pltpu.VMEM((2, HEAD_W, tq), BF16),
                        pltpu.VMEM((n_q, HEAD_W, tq), BF16),
                        pltpu.VMEM((3, 2, tq, tq), F32),
                        pltpu.VMEM((3, 2, 1, tq), F32),
                        pltpu.VMEM((2, 2, tq, tq), BF16),
                        pltpu.VMEM((2, 2, 1, tq), F32),
                        pltpu.VMEM((2, 1, tq), F32),
                        pltpu.VMEM((2, 1, tq), F32),
                        pltpu.VMEM((2, HEAD_W, tq), F32)],
        compiler_params=_cparams("arbitrary", "arbitrary"),
        name="diff_attention",
    )(lam_vecs, subln_g.reshape(HEAD_W, 1), q, q, k, v, gate)


def _conv_body(val_ref, glu_ref, pval_ref, pglu_ref, gate_ref, dww_ref, dwb_ref, lng_ref,
               lnb_ref, pww_ref, pwb_ref, o_ref, ext_ref, sh_ref, y_ref, *, rc, cc):
    i = pl.program_id(0)
    T, C = y_ref.shape
    L = T + CONV_HALO
    prev = pval_ref[...].astype(F32) * pglu_ref[...].astype(F32)
    ext_ref[0:CONV_HALO, :] = jnp.where(i > 0, prev, 0.0)
    ext_ref[CONV_HALO:L, :] = val_ref[...].astype(F32) * glu_ref[...].astype(F32)
    ext_ref[L:L + SUBLANES, :] = jnp.zeros((SUBLANES, C), F32)

    for s in range(1, SUBLANES):
        for c0 in range(0, C, cc):
            sh_ref[s - 1, :, c0:c0 + cc] = ext_ref[s:s + L, c0:c0 + cc]

    def window(off, r0, c0):
        s, whole = off % SUBLANES, off - off % SUBLANES
        rows = slice(r0 + whole, r0 + whole + rc)
        return ext_ref[rows, c0:c0 + cc] if s == 0 else sh_ref[s - 1, rows, c0:c0 + cc]

    base = CONV_HALO - (CONV_K - 1)
    for r0 in range(0, T, rc):
        for c0 in range(0, C, cc):
            acc = jnp.zeros((rc, cc), F32) + dwb_ref[:, c0:c0 + cc]
            for j in range(CONV_K):
                acc = acc + dww_ref[j:j + 1, c0:c0 + cc] * window(base + j, r0, c0)
            y_ref[r0:r0 + rc, c0:c0 + cc] = acc

    a = _silu(_layer_norm_rows(y_ref[...], lng_ref[...], lnb_ref[...]))
    z = jnp.dot(a.astype(BF16), pww_ref[...], preferred_element_type=F32) + pwb_ref[...]
    o_ref[...] = (z * gate_ref[...].astype(F32)).astype(o_ref.dtype)


def _conv_branch(val, glu, gate, dw_w, dw_b, ln_g, ln_b, pw_wb, layer, pw_b, *, T=256):
    S, C = val.shape
    T = min(T, S)
    hb = T // CONV_HALO
    row = lambda a: a.reshape(1, C)
    prev_map = lambda i: (jnp.maximum(i * hb - 1, 0), 0)
    return pl.pallas_call(
        functools.partial(_conv_body, rc=64, cc=256),
        grid=(S // T,),
        in_specs=[pl.BlockSpec((T, C), lambda i: (i, 0)),
                  pl.BlockSpec((T, C), lambda i: (i, 0)),
                  pl.BlockSpec((CONV_HALO, C), prev_map),
                  pl.BlockSpec((CONV_HALO, C), prev_map),
                  pl.BlockSpec((T, C), lambda i: (i, 0)),
                  pl.BlockSpec((CONV_K, C), lambda i: (0, 0)),
                  pl.BlockSpec((1, C), lambda i: (0, 0)),
                  pl.BlockSpec((1, C), lambda i: (0, 0)),
                  pl.BlockSpec((1, C), lambda i: (0, 0)),
                  pl.BlockSpec((None, C, C), lambda i: (layer, 0, 0)),
                  pl.BlockSpec((1, C), lambda i: (0, 0))],
        out_specs=pl.BlockSpec((T, C), lambda i: (i, 0)),
        out_shape=jax.ShapeDtypeStruct((S, C), BF16),
        scratch_shapes=[pltpu.VMEM((T + CONV_HALO + SUBLANES, C), F32),
                        pltpu.VMEM((SUBLANES - 1, T + CONV_HALO, C), F32),
                        pltpu.VMEM((T, C), F32)],
        compiler_params=_cparams("parallel"),
        name="conv_branch",
    )(val, glu, val, glu, gate, dw_w, row(dw_b), row(ln_g), row(ln_b), pw_wb, row(pw_b))


def _sgu_body(u_ref, v_ref, gate_ref, lng_ref, lnb_ref, w_ref, b_ref, o_ref, vn_ref):
    T, W = vn_ref.shape
    vn_ref[...] = _layer_norm_rows(v_ref[...].astype(F32), lng_ref[...], lnb_ref[...]).astype(BF16)
    r_id = lax.broadcasted_iota(jnp.int32, (CHUNK, CHUNK), 0)
    c_id = lax.broadcasted_iota(jnp.int32, (CHUNK, CHUNK), 1)
    causal = c_id <= r_id
    for g in range(W // SGU_GROUP_W):
        cols = slice(g * SGU_GROUP_W, (g + 1) * SGU_GROUP_W)
        wg = jnp.where(causal, w_ref[g], 0.0).astype(BF16)
        bias = b_ref[g]
        for n in range(T // CHUNK):
            rows = slice(n * CHUNK, (n + 1) * CHUNK)
            mixed = jnp.dot(wg, vn_ref[rows, cols], preferred_element_type=F32) + bias
            o_ref[rows, cols] = (u_ref[rows, cols].astype(F32) * mixed
                                 * gate_ref[rows, cols].astype(F32)).astype(o_ref.dtype)


def _sgu_branch(uv, gate, ln_g, ln_b, w_s, b_s, *, T=512):
    S = uv.shape[0]
    G = w_s.shape[0]
    W = G * SGU_GROUP_W
    T = min(T, S)
    return pl.pallas_call(
        _sgu_body,
        grid=(S // T,),
        in_specs=[pl.BlockSpec((T, W), lambda i: (i, 0)),
                  pl.BlockSpec((T, W), lambda i: (i, 1)),
                  pl.BlockSpec((T, W), lambda i: (i, 0)),
                  pl.BlockSpec((1, W), lambda i: (0, 0)),
                  pl.BlockSpec((1, W), lambda i: (0, 0)),
                  pl.BlockSpec((G, CHUNK, CHUNK), lambda i: (0, 0, 0)),
                  pl.BlockSpec((G, CHUNK, 1), lambda i: (0, 0, 0))],
        out_specs=pl.BlockSpec((T, W), lambda i: (i, 0)),
        out_shape=jax.ShapeDtypeStruct((S, W), BF16),
        scratch_shapes=[pltpu.VMEM((T, W), BF16)],
        compiler_params=_cparams("parallel"),
        name="sgu_branch",
    )(uv, uv, gate, ln_g.reshape(1, W), ln_b.reshape(1, W), w_s, b_s.reshape(G, CHUNK, 1))


def _outproj_body(a_ref, c_ref, s_ref, wa_ref, wc_ref, ws_ref, x_ref, o_ref, *, alpha):
    y = jnp.dot(a_ref[...], wa_ref[...], preferred_element_type=F32)
    y = y + jnp.dot(c_ref[...], wc_ref[...], preferred_element_type=F32)
    y = y + jnp.dot(s_ref[...], ws_ref[...], preferred_element_type=F32)
    o_ref[...] = alpha * x_ref[...] + y


def _out_projection(attn, conv, sgu, w, layer, x, *, alpha, tm=1024, tn=1024):
    S, D = x.shape
    tm = min(tm, S)
    wa, wc, ws = attn.shape[1], conv.shape[1], sgu.shape[1]
    return pl.pallas_call(
        functools.partial(_outproj_body, alpha=alpha),
        grid=(S // tm, D // tn),
        in_specs=[pl.BlockSpec((tm, wa), lambda i, j: (i, 0)),
                  pl.BlockSpec((tm, wc), lambda i, j: (i, 0)),
                  pl.BlockSpec((tm, ws), lambda i, j: (i, 0)),
                  pl.BlockSpec((None, wa, tn), lambda i, j: (layer, 0, j)),
                  pl.BlockSpec((None, wc, tn), lambda i, j: (layer, wa // wc, j)),
                  pl.BlockSpec((None, ws, tn), lambda i, j: (layer, (wa + wc) // ws, j)),
                  pl.BlockSpec((tm, tn), lambda i, j: (i, j))],
        out_specs=pl.BlockSpec((tm, tn), lambda i, j: (i, j)),
        out_shape=jax.ShapeDtypeStruct((S, D), F32),
        compiler_params=_cparams("parallel", "arbitrary"),
        name="out_projection",
    )(attn, conv, sgu, w, w, w, x)


def _postln_body(r_ref, g_ref, b_ref, o_ref, ob_ref):
    y = _layer_norm_rows(r_ref[...], g_ref[...], b_ref[...])
    o_ref[...] = y
    ob_ref[...] = y.astype(ob_ref.dtype)


def _post_layer_norm(r, g, b, *, tr=256):
    S, D = r.shape
    tr = min(tr, S)
    return pl.pallas_call(
        _postln_body,
        grid=(S // tr,),
        in_specs=[pl.BlockSpec((tr, D), lambda i: (i, 0)),
                  pl.BlockSpec((1, D), lambda i: (0, 0)),
                  pl.BlockSpec((1, D), lambda i: (0, 0))],
        out_specs=[pl.BlockSpec((tr, D), lambda i: (i, 0))] * 2,
        out_shape=[jax.ShapeDtypeStruct((S, D), F32), jax.ShapeDtypeStruct((S, D), BF16)],
        compiler_params=_cparams("parallel"),
        name="post_layer_norm",
    )(r, g.reshape(1, D), b.reshape(1, D))


def kernel(x, positions, w_in, attn_lambda, attn_subln_g, conv_dw_w, conv_dw_b, conv_ln_g, conv_ln_b, conv_pw_w, conv_pw_b, sgu_ln_g, sgu_ln_b, sgu_w, sgu_b, w_out, post_ln_g, post_ln_b):
    B, S, D = x.shape
    depth = w_in.shape[0]
    assert B == 1 and D % 4 == 0
    u = D // 4
    alpha = (2 * depth) ** 0.25
    segments = (("q", 0, 2, "rope_q"), ("k", 2, 2, "rope_k"), ("v", 4, 2, "ident"),
                ("a_gate", 6, 2, "silu"), ("c_val", 8, 1, "ident"), ("c_glu", 9, 1, "sigmoid"),
                ("c_gate", 10, 1, "silu"), ("s_uv", 11, 2, "gelu"), ("s_gate", 13, 1, "silu"))

    cos, sin = _rope_tables(positions)
    xf = x.reshape(S, D)
    xb = _to_bf16(x)[0]
    w_out_b, pw_w_b = _to_bf16(w_out), _to_bf16(conv_pw_w)
    for layer in range(depth):
        lambda_init = 0.8 - 0.6 * math.exp(-0.3 * layer)
        h = {name: _in_projection(xb, w_in, layer, cos, sin, kind=kind, col0=first * u,
                                  ncols=n * u)
             for name, first, n, kind in segments}
        attn = _attention(h["q"], h["k"], h["v"], h["a_gate"], attn_lambda[layer],
                          attn_subln_g[layer], lambda_init=lambda_init)
        conv = _conv_branch(h["c_val"], h["c_glu"], h["c_gate"], conv_dw_w[layer],
                            conv_dw_b[layer], conv_ln_g[layer], conv_ln_b[layer],
                            pw_w_b, layer, conv_pw_b[layer])
        sgu = _sgu_branch(h["s_uv"], h["s_gate"], sgu_ln_g[layer], sgu_ln_b[layer],
                          sgu_w[layer], sgu_b[layer])
        r = _out_projection(attn, conv, sgu, w_out_b, layer, xf, alpha=alpha)
        xf, xb = _post_layer_norm(r, post_ln_g[layer], post_ln_b[layer])
    return xf.reshape(B, S, D)
```

```python
import functools
import math

import jax
import jax.numpy as jnp
from jax import lax
from jax.experimental import pallas as pl
from jax.experimental.pallas import tpu as pltpu

F32 = jnp.float32
BF16 = jnp.bfloat16

HEAD_DIM = 64
HEAD_W = 2 * HEAD_DIM
CONV_K = 31
CONV_HALO = 32
CHUNK = 128
SGU_GROUP_W = 128
ROPE_THETA = 10000.0
LN_EPS = 1e-5
RMS_EPS = 1e-5
MASK_VALUE = -1e30
Q_SCALE = HEAD_DIM ** -0.5 * math.log2(math.e)
SUBLANES = 8
VMEM_LIMIT = 56 * 1024 * 1024


def _cparams(*sem):
    return pltpu.CompilerParams(dimension_semantics=sem, vmem_limit_bytes=VMEM_LIMIT)


def _layer_norm_rows(x, g, b):
    mu = jnp.mean(x, axis=-1, keepdims=True)
    xc = x - mu
    var = jnp.mean(xc * xc, axis=-1, keepdims=True)
    return xc * lax.rsqrt(var + LN_EPS) * g + b


def _silu(x):
    return x * jax.nn.sigmoid(x)


def _gelu(x):
    return 0.5 * x * (1.0 + lax.erf(x * (1.0 / math.sqrt(2.0))))


def _rope_body(pos_ref, inv_ref, cos_ref, sin_ref):
    ang = pos_ref[...].astype(F32) * inv_ref[...]
    lane = lax.broadcasted_iota(jnp.int32, ang.shape, 1)
    first_half = (lane % HEAD_DIM) < (HEAD_DIM // 2)
    s = jnp.sin(ang)
    cos_ref[...] = jnp.cos(ang)
    sin_ref[...] = jnp.where(first_half, -s, s)


def _rope_tables(positions):
    S = positions.shape[-1]
    ts = min(S, 1024)
    inv = ROPE_THETA ** (-jnp.arange(0, HEAD_DIM, 2, dtype=F32) / HEAD_DIM)
    inv = jnp.tile(inv, HEAD_W // (HEAD_DIM // 2))[None, :]
    return pl.pallas_call(
        _rope_body,
        grid=(S // ts,),
        in_specs=[pl.BlockSpec((ts, 1), lambda i: (i, 0)),
                  pl.BlockSpec((1, HEAD_W), lambda i: (0, 0))],
        out_specs=[pl.BlockSpec((ts, HEAD_W), lambda i: (i, 0))] * 2,
        out_shape=[jax.ShapeDtypeStruct((S, HEAD_W), F32)] * 2,
        compiler_params=_cparams("parallel"),
        name="rope_tables",
    )(positions.reshape(S, 1), inv)


def _rope(t, cos, sin, scale):
    lane = lax.broadcasted_iota(jnp.int32, t.shape, 1)
    first_half = (lane % HEAD_DIM) < (HEAD_DIM // 2)
    rot = jnp.where(first_half,
                    pltpu.roll(t, HEAD_W - HEAD_DIM // 2, 1),
                    pltpu.roll(t, HEAD_DIM // 2, 1))
    return (t * cos + rot * sin) * scale


_POINTWISE = {"ident": lambda t: t, "silu": _silu, "sigmoid": jax.nn.sigmoid, "gelu": _gelu}
_ROPE_SCALE = {"rope_q": Q_SCALE, "rope_k": 1.0}


def _inproj_body(x_ref, w_ref, cos_ref, sin_ref, o_ref, wb_ref, *, kind, rm, kchunk):
    tm, tn = o_ref.shape

    @pl.when(pl.program_id(1) == 0)
    def _():
        for k0 in range(0, w_ref.shape[0], kchunk):
            wb_ref[k0:k0 + kchunk, :] = w_ref[k0:k0 + kchunk, :].astype(wb_ref.dtype)

    for r0 in range(0, tm, rm):
        rows = slice(r0, r0 + rm)
        a = jnp.dot(x_ref[rows, :], wb_ref[...], preferred_element_type=F32)
        if kind in _ROPE_SCALE:
            cos, sin = cos_ref[rows, :], sin_ref[rows, :]
            for c0 in range(0, tn, HEAD_W):
                o_ref[rows, c0:c0 + HEAD_W] = _rope(a[:, c0:c0 + HEAD_W], cos, sin,
                                                    _ROPE_SCALE[kind]).astype(o_ref.dtype)
        else:
            o_ref[rows, :] = _POINTWISE[kind](a).astype(o_ref.dtype)


def _cast_body(x_ref, o_ref):
    o_ref[...] = x_ref[...].astype(o_ref.dtype)


def _to_bf16(a, *, block_bytes=8 * 1024 * 1024):
    L, R, C = a.shape
    tr = R
    while tr * C * 4 > block_bytes and tr % 16 == 0:
        tr //= 2
    return pl.pallas_call(
        _cast_body,
        grid=(L, R // tr),
        in_specs=[pl.BlockSpec((None, tr, C), lambda l, i: (l, i, 0))],
        out_specs=pl.BlockSpec((None, tr, C), lambda l, i: (l, i, 0)),
        out_shape=jax.ShapeDtypeStruct(a.shape, BF16),
        compiler_params=_cparams("parallel", "parallel"),
        name="to_bf16",
    )(a)


def _in_projection(xb, w, layer, cos, sin, *, kind, col0, ncols, tm=1024, tn=512, rm=256):
    S, D = xb.shape
    tm = min(tm, S)
    j0 = col0 // tn
    return pl.pallas_call(
        functools.partial(_inproj_body, kind=kind, rm=rm, kchunk=512),
        grid=(ncols // tn, S // tm),
        in_specs=[pl.BlockSpec((tm, D), lambda j, i: (i, 0)),
                  pl.BlockSpec((None, D, tn), lambda j, i: (layer, 0, j0 + j)),
                  pl.BlockSpec((tm, HEAD_W), lambda j, i: (i, 0)),
                  pl.BlockSpec((tm, HEAD_W), lambda j, i: (i, 0))],
        out_specs=pl.BlockSpec((tm, tn), lambda j, i: (i, j)),
        out_shape=jax.ShapeDtypeStruct((S, ncols), BF16),
        scratch_shapes=[pltpu.VMEM((D, tn), BF16)],
        compiler_params=_cparams("arbitrary", "arbitrary"),
        name="in_projection_" + kind,
    )(xb, w, cos, sin)


FIRST = 2


def _attn_body(lam_ref, g_ref, q_ref, qn_ref, k_ref, v_ref, gate_ref, o_ref,
               qt_ref, vt_ref, st_ref, bm_ref, pt_ref, al_ref, m_ref, l_ref, acc_ref,
               *, tq, tk, rc, lambda_init):
    qi = pl.program_id(1)
    halves = range(2)
    n_ch = tk // rc
    kc = 2 * rc
    assert tk % kc == 0 and tq == 2 * tk

    def by_vreg(a):
        return a.reshape(a.shape[0] // SUBLANES, SUBLANES, a.shape[1])

    all_q = slice(0, tq)

    def score(b, slot, qs=all_q):
        k = k_ref[pl.ds(pl.multiple_of(b * tk, tk), tk), :]
        bm = []
        for c in halves:
            st = jnp.dot(k, qt_ref[c, :, qs], preferred_element_type=F32)
            st_ref[slot, c, :, qs] = st
            bm.append(jnp.max(by_vreg(st), axis=0))
        return bm

    def score_end(slot, bm, qs=all_q):
        for c in halves:
            bm_ref[slot, c, :, qs] = jnp.max(bm[c], axis=0, keepdims=True)

    def load_scores(slot, c, r, masked, qs):
        st = st_ref[slot, c, r * rc:(r + 1) * rc, qs]
        if masked is not None:
            k_id = lax.broadcasted_iota(jnp.int32, st.shape, 0) + (r * rc + masked)
            q_id = lax.broadcasted_iota(jnp.int32, st.shape, 1) + qs.start
            st = jnp.where(k_id <= q_id, st, MASK_VALUE)
        return st

    def prob_begin(slot, masked, qs):
        if masked is not None:
            bmax = []
            for c in halves:
                bm = jnp.max(by_vreg(load_scores(slot, c, 0, masked, qs)), axis=0)
                for r in range(1, n_ch):
                    bm = jnp.maximum(bm, jnp.max(by_vreg(load_scores(slot, c, r, masked, qs)),
                                                 axis=0))
                bmax.append(jnp.max(bm, axis=0, keepdims=True))
        else:
            bmax = [bm_ref[slot, c, :, qs] for c in halves]
        m_prev = [m_ref[c, :, qs] for c in halves]
        m_cur = [jnp.maximum(m_prev[c], bmax[c]) for c in halves]
        alpha = [jnp.exp2(m_prev[c] - m_cur[c]) for c in halves]
        return m_cur, alpha

    def prob_chunk(slot, par, r, masked, qs, m_cur, psum):
        for c in halves:
            p = jnp.exp2(load_scores(slot, c, r, masked, qs) - m_cur[c])
            pt_ref[par, c, r * rc:(r + 1) * rc, qs] = p.astype(pt_ref.dtype)
            ps = jnp.sum(by_vreg(p), axis=0)
            psum[c] = ps if psum[c] is None else psum[c] + ps

    def prob_end(par, qs, m_cur, alpha, psum):
        for c in halves:
            al_ref[par, c, :, qs] = alpha[c]
            l_ref[c, :, qs] = alpha[c] * l_ref[c, :, qs] + jnp.sum(psum[c], axis=0, keepdims=True)
            m_ref[c, :, qs] = m_cur[c]

    def value_chunk(b, par, j, qs, part):
        keys = slice(j * kc, (j + 1) * kc)
        vt = vt_ref[b, :, keys]
        for c in halves:
            pv = jnp.dot(vt, pt_ref[par, c, keys, qs], preferred_element_type=F32)
            part[c] = pv if part[c] is None else part[c] + pv

    def value_end(par, qs, part):
        for c in halves:
            acc_ref[c, :, qs] = al_ref[par, c, :, qs] * acc_ref[c, :, qs] + part[c]

    def consume(b, slot, par, masked, qs=all_q, next_slot=None, next_qs=all_q):
        m_cur, alpha = prob_begin(slot, masked, qs)
        psum, part = [None, None], [None, None]
        bm = score(b + 1, next_slot, next_qs) if next_slot is not None else None
        for r in range(n_ch):
            prob_chunk(slot, par, r, masked, qs, m_cur, psum)
            if (r + 1) * rc % kc == 0:
                value_chunk(b, par, (r + 1) * rc // kc - 1, qs, part)
        if bm is not None:
            score_end(next_slot, bm, next_qs)
        prob_end(par, qs, m_cur, alpha, psum)
        value_end(par, qs, part)

    def start_block(ref):
        qt = ref[...].astype(F32).T
        row = lax.broadcasted_iota(jnp.int32, qt.shape, 0)
        qt_ref[0] = jnp.where(row < HEAD_DIM, qt, 0.0).astype(qt_ref.dtype)
        qt_ref[1] = jnp.where(row >= HEAD_DIM, qt, 0.0).astype(qt_ref.dtype)
        score_end(FIRST, score(0, FIRST))

    @pl.when(qi == 0)
    def _():
        for kb in range(vt_ref.shape[0]):
            vt_ref[kb] = v_ref[kb * tk:(kb + 1) * tk, :].astype(F32).T.astype(vt_ref.dtype)
        start_block(q_ref)

    m_ref[...] = jnp.full(m_ref.shape, MASK_VALUE, F32)
    l_ref[...] = jnp.zeros(l_ref.shape, F32)
    acc_ref[...] = jnp.zeros(acc_ref.shape, F32)

    @pl.when(qi > 0)
    def _():
        consume(0, FIRST, 0, None, next_slot=1)

    def body(b, carry):
        lax.cond(b % 2 == 0,
                 lambda: consume(b, 0, 0, None, next_slot=1),
                 lambda: consume(b, 1, 1, None, next_slot=0))
        return carry

    lax.fori_loop(1, 2 * qi, body, 0)

    upper_q = slice(tk, tq)
    for cond, slot in ((qi == 0, FIRST), (qi > 0, 0)):
        @pl.when(cond)
        def _():
            consume(2 * qi, slot, 0, 0, next_slot=1, next_qs=upper_q)

    start_block(qn_ref)
    consume(2 * qi + 1, 1, 1, tk, qs=upper_q)

    lv = lam_ref[...]
    lam = (jnp.exp(jnp.sum(lv[0:1] * lv[1:2], axis=1, keepdims=True))
           - jnp.exp(jnp.sum(lv[2:3] * lv[3:4], axis=1, keepdims=True)) + lambda_init)
    ot = acc_ref[0] * (1.0 / l_ref[0]) - lam * (acc_ref[1] * (1.0 / l_ref[1]))
    ot = ot * lax.rsqrt(jnp.mean(ot * ot, axis=0, keepdims=True) + RMS_EPS)
    ot = ot * g_ref[...] * (1.0 - lambda_init)
    o_ref[...] = (ot.T * gate_ref[...].astype(F32)).astype(o_ref.dtype)


def _attention(q, k, v, gate, lam_vecs, subln_g, *, lambda_init, tq=1024):
    S = q.shape[0]
    tq = min(tq, S)
    tk = tq // 2
    n_q = S // tq
    H = q.shape[1] // HEAD_W
    return pl.pallas_call(
        functools.partial(_attn_body, tq=tq, tk=tk, rc=min(128, tk // 2),
                          lambda_init=lambda_init),
        grid=(H, n_q),
        in_specs=[pl.BlockSpec((4, HEAD_DIM), lambda hh, i: (0, 0)),
                  pl.BlockSpec((HEAD_W, 1), lambda hh, i: (0, 0)),
                  pl.BlockSpec((tq, HEAD_W), lambda hh, i: (i, hh)),
                  pl.BlockSpec((tq, HEAD_W),
                               lambda hh, i: (jnp.minimum(i + 1, n_q - 1), hh)),
                  pl.BlockSpec((S, HEAD_W), lambda hh, i: (0, hh)),
                  pl.BlockSpec((S, HEAD_W), lambda hh, i: (0, hh)),
                  pl.BlockSpec((tq, HEAD_W), lambda hh, i: (i, hh))],
        out_specs=pl.BlockSpec((tq, HEAD_W), lambda hh, i: (i, hh)),
        out_shape=jax.ShapeDtypeStruct((S, H * HEAD_W), BF16),
        scratch_shapes=[pltpu.VMEM((2, HEAD_W, tq), BF16),
                        pltpu.VMEM((S // tk, HEAD_W, tk), BF16),
                        pltpu.VMEM((3, 2, tk, tq), F32),
                        pltpu.VMEM((3, 2, 1, tq), F32),
                        pltpu.VMEM((2, 2, tk, tq), BF16),
                        pltpu.VMEM((2, 2, 1, tq), F32),
                        pltpu.VMEM((2, 1, tq), F32),
                        pltpu.VMEM((2, 1, tq), F32),
                        pltpu.VMEM((2, HEAD_W, tq), F32)],
        compiler_params=_cparams("arbitrary", "arbitrary"),
        name="diff_attention",
    )(lam_vecs, subln_g.reshape(HEAD_W, 1), q, q, k, v, gate)


def _conv_body(val_ref, glu_ref, pval_ref, pglu_ref, gate_ref, dww_ref, dwb_ref, lng_ref,
               lnb_ref, pww_ref, pwb_ref, o_ref, ext_ref, sh_ref, y_ref, *, rc, cc):
    i = pl.program_id(0)
    T, C = y_ref.shape
    L = T + CONV_HALO
    prev = pval_ref[...].astype(F32) * pglu_ref[...].astype(F32)
    ext_ref[0:CONV_HALO, :] = jnp.where(i > 0, prev, 0.0)
    ext_ref[CONV_HALO:L, :] = val_ref[...].astype(F32) * glu_ref[...].astype(F32)
    ext_ref[L:L + SUBLANES, :] = jnp.zeros((SUBLANES, C), F32)

    for s in range(1, SUBLANES):
        for c0 in range(0, C, cc):
            sh_ref[s - 1, :, c0:c0 + cc] = ext_ref[s:s + L, c0:c0 + cc]

    def window(off, r0, c0):
        s, whole = off % SUBLANES, off - off % SUBLANES
        rows = slice(r0 + whole, r0 + whole + rc)
        return ext_ref[rows, c0:c0 + cc] if s == 0 else sh_ref[s - 1, rows, c0:c0 + cc]

    base = CONV_HALO - (CONV_K - 1)
    for r0 in range(0, T, rc):
        for c0 in range(0, C, cc):
            acc = jnp.zeros((rc, cc), F32) + dwb_ref[:, c0:c0 + cc]
            for j in range(CONV_K):
                acc = acc + dww_ref[j:j + 1, c0:c0 + cc] * window(base + j, r0, c0)
            y_ref[r0:r0 + rc, c0:c0 + cc] = acc

    a = _silu(_layer_norm_rows(y_ref[...], lng_ref[...], lnb_ref[...]))
    z = jnp.dot(a.astype(BF16), pww_ref[...], preferred_element_type=F32) + pwb_ref[...]
    o_ref[...] = (z * gate_ref[...].astype(F32)).astype(o_ref.dtype)


def _conv_branch(val, glu, gate, dw_w, dw_b, ln_g, ln_b, pw_wb, layer, pw_b, *, T=256):
    S, C = val.shape
    T = min(T, S)
    hb = T // CONV_HALO
    row = lambda a: a.reshape(1, C)
    prev_map = lambda i: (jnp.maximum(i * hb - 1, 0), 0)
    return pl.pallas_call(
        functools.partial(_conv_body, rc=64, cc=256),
        grid=(S // T,),
        in_specs=[pl.BlockSpec((T, C), lambda i: (i, 0)),
                  pl.BlockSpec((T, C), lambda i: (i, 0)),
                  pl.BlockSpec((CONV_HALO, C), prev_map),
                  pl.BlockSpec((CONV_HALO, C), prev_map),
                  pl.BlockSpec((T, C), lambda i: (i, 0)),
                  pl.BlockSpec((CONV_K, C), lambda i: (0, 0)),
                  pl.BlockSpec((1, C), lambda i: (0, 0)),
                  pl.BlockSpec((1, C), lambda i: (0, 0)),
                  pl.BlockSpec((1, C), lambda i: (0, 0)),
                  pl.BlockSpec((None, C, C), lambda i: (layer, 0, 0)),
                  pl.BlockSpec((1, C), lambda i: (0, 0))],
        out_specs=pl.BlockSpec((T, C), lambda i: (i, 0)),
        out_shape=jax.ShapeDtypeStruct((S, C), BF16),
        scratch_shapes=[pltpu.VMEM((T + CONV_HALO + SUBLANES, C), F32),
                        pltpu.VMEM((SUBLANES - 1, T + CONV_HALO, C), F32),
                        pltpu.VMEM((T, C), F32)],
        compiler_params=_cparams("parallel"),
        name="conv_branch",
    )(val, glu, val, glu, gate, dw_w, row(dw_b), row(ln_g), row(ln_b), pw_wb, row(pw_b))


def _sgu_body(u_ref, v_ref, gate_ref, lng_ref, lnb_ref, w_ref, b_ref, o_ref, vn_ref):
    T, W = vn_ref.shape
    vn_ref[...] = _layer_norm_rows(v_ref[...].astype(F32), lng_ref[...], lnb_ref[...]).astype(BF16)
    r_id = lax.broadcasted_iota(jnp.int32, (CHUNK, CHUNK), 0)
    c_id = lax.broadcasted_iota(jnp.int32, (CHUNK, CHUNK), 1)
    causal = c_id <= r_id
    for g in range(W // SGU_GROUP_W):
        cols = slice(g * SGU_GROUP_W, (g + 1) * SGU_GROUP_W)
        wg = jnp.where(causal, w_ref[g], 0.0).astype(BF16)
        bias = b_ref[g]
        for n in range(T // CHUNK):
            rows = slice(n * CHUNK, (n + 1) * CHUNK)
            mixed = jnp.dot(wg, vn_ref[rows, cols], preferred_element_type=F32) + bias
            o_ref[rows, cols] = (u_ref[rows, cols].astype(F32) * mixed
                                 * gate_ref[rows, cols].astype(F32)).astype(o_ref.dtype)


def _sgu_branch(uv, gate, ln_g, ln_b, w_s, b_s, *, T=512):
    S = uv.shape[0]
    G = w_s.shape[0]
    W = G * SGU_GROUP_W
    T = min(T, S)
    return pl.pallas_call(
        _sgu_body,
        grid=(S // T,),
        in_specs=[pl.BlockSpec((T, W), lambda i: (i, 0)),
                  pl.BlockSpec((T, W), lambda i: (i, 1)),
                  pl.BlockSpec((T, W), lambda i: (i, 0)),
                  pl.BlockSpec((1, W), lambda i: (0, 0)),
                  pl.BlockSpec((1, W), lambda i: (0, 0)),
                  pl.BlockSpec((G, CHUNK, CHUNK), lambda i: (0, 0, 0)),
                  pl.BlockSpec((G, CHUNK, 1), lambda i: (0, 0, 0))],
        out_specs=pl.BlockSpec((T, W), lambda i: (i, 0)),
        out_shape=jax.ShapeDtypeStruct((S, W), BF16),
        scratch_shapes=[pltpu.VMEM((T, W), BF16)],
        compiler_params=_cparams("parallel"),
        name="sgu_branch",
    )(uv, uv, gate, ln_g.reshape(1, W), ln_b.reshape(1, W), w_s, b_s.reshape(G, CHUNK, 1))


def _outproj_body(a_ref, c_ref, s_ref, wa_ref, wc_ref, ws_ref, x_ref, o_ref, *, alpha):
    y = jnp.dot(a_ref[...], wa_ref[...], preferred_element_type=F32)
    y = y + jnp.dot(c_ref[...], wc_ref[...], preferred_element_type=F32)
    y = y + jnp.dot(s_ref[...], ws_ref[...], preferred_element_type=F32)
    o_ref[...] = alpha * x_ref[...] + y


def _out_projection(attn, conv, sgu, w, layer, x, *, alpha, tm=1024, tn=1024):
    S, D = x.shape
    tm = min(tm, S)
    wa, wc, ws = attn.shape[1], conv.shape[1], sgu.shape[1]
    return pl.pallas_call(
        functools.partial(_outproj_body, alpha=alpha),
        grid=(S // tm, D // tn),
        in_specs=[pl.BlockSpec((tm, wa), lambda i, j: (i, 0)),
                  pl.BlockSpec((tm, wc), lambda i, j: (i, 0)),
                  pl.BlockSpec((tm, ws), lambda i, j: (i, 0)),
                  pl.BlockSpec((None, wa, tn), lambda i, j: (layer, 0, j)),
                  pl.BlockSpec((None, wc, tn), lambda i, j: (layer, wa // wc, j)),
                  pl.BlockSpec((None, ws, tn), lambda i, j: (layer, (wa + wc) // ws, j)),
                  pl.BlockSpec((tm, tn), lambda i, j: (i, j))],
        out_specs=pl.BlockSpec((tm, tn), lambda i, j: (i, j)),
        out_shape=jax.ShapeDtypeStruct((S, D), F32),
        compiler_params=_cparams("parallel", "arbitrary"),
        name="out_projection",
    )(attn, conv, sgu, w, w, w, x)


def _postln_body(r_ref, g_ref, b_ref, *o_refs):
    y = _layer_norm_rows(r_ref[...], g_ref[...], b_ref[...])
    for o_ref in o_refs:
        o_ref[...] = y.astype(o_ref.dtype)


def _post_layer_norm(r, g, b, *, with_bf16, tr=256):
    S, D = r.shape
    tr = min(tr, S)
    dtypes = (F32, BF16) if with_bf16 else (F32,)
    return pl.pallas_call(
        _postln_body,
        grid=(S // tr,),
        in_specs=[pl.BlockSpec((tr, D), lambda i: (i, 0)),
                  pl.BlockSpec((1, D), lambda i: (0, 0)),
                  pl.BlockSpec((1, D), lambda i: (0, 0))],
        out_specs=[pl.BlockSpec((tr, D), lambda i: (i, 0)) for _ in dtypes],
        out_shape=[jax.ShapeDtypeStruct((S, D), dt) for dt in dtypes],
        compiler_params=_cparams("parallel"),
        name="post_layer_norm",
    )(r, g.reshape(1, D), b.reshape(1, D))


def kernel(x, positions, w_in, attn_lambda, attn_subln_g, conv_dw_w, conv_dw_b, conv_ln_g, conv_ln_b, conv_pw_w, conv_pw_b, sgu_ln_g, sgu_ln_b, sgu_w, sgu_b, w_out, post_ln_g, post_ln_b):
    B, S, D = x.shape
    depth = w_in.shape[0]
    assert B == 1 and D % 4 == 0
    u = D // 4
    alpha = (2 * depth) ** 0.25
    segments = (("q", 0, 2, "rope_q"), ("k", 2, 2, "rope_k"), ("v", 4, 2, "ident"),
                ("a_gate", 6, 2, "silu"), ("c_val", 8, 1, "ident"), ("c_glu", 9, 1, "sigmoid"),
                ("c_gate", 10, 1, "silu"), ("s_uv", 11, 2, "gelu"), ("s_gate", 13, 1, "silu"))

    cos, sin = _rope_tables(positions)
    xf = x.reshape(S, D)
    xb = _to_bf16(x)[0]
    w_out_b, pw_w_b = _to_bf16(w_out), _to_bf16(conv_pw_w)
    for layer in range(depth):
        lambda_init = 0.8 - 0.6 * math.exp(-0.3 * layer)
        h = {name: _in_projection(xb, w_in, layer, cos, sin, kind=kind, col0=first * u,
                                  ncols=n * u)
             for name, first, n, kind in segments}
        attn = _attention(h["q"], h["k"], h["v"], h["a_gate"], attn_lambda[layer],
                          attn_subln_g[layer], lambda_init=lambda_init)
        conv = _conv_branch(h["c_val"], h["c_glu"], h["c_gate"], conv_dw_w[layer],
                            conv_dw_b[layer], conv_ln_g[layer], conv_ln_b[layer],
                            pw_w_b, layer, conv_pw_b[layer])
        sgu = _sgu_branch(h["s_uv"], h["s_gate"], sgu_ln_g[layer], sgu_ln_b[layer],
                          sgu_w[layer], sgu_b[layer])
        r = _out_projection(attn, conv, sgu, w_out_b, layer, xf, alpha=alpha)
        outs = _post_layer_norm(r, post_ln_g[layer], post_ln_b[layer],
                                with_bf16=layer + 1 < depth)
        xf, xb = outs[0], outs[-1]
    return xf.reshape(B, S, D)
```

```python
import functools
import math

import jax
import jax.numpy as jnp
from jax import lax
from jax.experimental import pallas as pl
from jax.experimental.pallas import tpu as pltpu

F32 = jnp.float32
BF16 = jnp.bfloat16

HEAD_DIM = 64
HEAD_W = 2 * HEAD_DIM
CONV_K = 31
CONV_HALO = 32
CHUNK = 128
SGU_GROUP_W = 128
ROPE_THETA = 10000.0
LN_EPS = 1e-5
RMS_EPS = 1e-5
MASK_VALUE = -1e30
Q_SCALE = HEAD_DIM ** -0.5 * math.log2(math.e)
SUBLANES = 8
VMEM_LIMIT = 56 * 1024 * 1024


def _cparams(*sem):
    return pltpu.CompilerParams(dimension_semantics=sem, vmem_limit_bytes=VMEM_LIMIT)


def _layer_norm_rows(x, g, b):
    mu = jnp.mean(x, axis=-1, keepdims=True)
    xc = x - mu
    var = jnp.mean(xc * xc, axis=-1, keepdims=True)
    return xc * lax.rsqrt(var + LN_EPS) * g + b


def _silu(x):
    return x * jax.nn.sigmoid(x)


def _gelu(x):
    return 0.5 * x * (1.0 + lax.erf(x * (1.0 / math.sqrt(2.0))))


def _rope_body(pos_ref, inv_ref, cos_ref, sin_ref):
    ang = pos_ref[...].astype(F32) * inv_ref[...]
    lane = lax.broadcasted_iota(jnp.int32, ang.shape, 1)
    first_half = (lane % HEAD_DIM) < (HEAD_DIM // 2)
    s = jnp.sin(ang)
    cos_ref[...] = jnp.cos(ang)
    sin_ref[...] = jnp.where(first_half, -s, s)


def _rope_tables(positions):
    S = positions.shape[-1]
    ts = min(S, 1024)
    inv = ROPE_THETA ** (-jnp.arange(0, HEAD_DIM, 2, dtype=F32) / HEAD_DIM)
    inv = jnp.tile(inv, HEAD_W // (HEAD_DIM // 2))[None, :]
    return pl.pallas_call(
        _rope_body,
        grid=(S // ts,),
        in_specs=[pl.BlockSpec((ts, 1), lambda i: (i, 0)),
                  pl.BlockSpec((1, HEAD_W), lambda i: (0, 0))],
        out_specs=[pl.BlockSpec((ts, HEAD_W), lambda i: (i, 0))] * 2,
        out_shape=[jax.ShapeDtypeStruct((S, HEAD_W), F32)] * 2,
        compiler_params=_cparams("parallel"),
        name="rope_tables",
    )(positions.reshape(S, 1), inv)


def _rope(t, cos, sin, scale):
    lane = lax.broadcasted_iota(jnp.int32, t.shape, 1)
    first_half = (lane % HEAD_DIM) < (HEAD_DIM // 2)
    rot = jnp.where(first_half,
                    pltpu.roll(t, HEAD_W - HEAD_DIM // 2, 1),
                    pltpu.roll(t, HEAD_DIM // 2, 1))
    return (t * cos + rot * sin) * scale


_POINTWISE = {"ident": lambda t: t, "silu": _silu, "sigmoid": jax.nn.sigmoid, "gelu": _gelu}
_ROPE_SCALE = {"rope_q": Q_SCALE, "rope_k": 1.0}


def _inproj_body(x_ref, w_ref, cos_ref, sin_ref, o_ref, wb_ref, *, kind, rm, kchunk):
    tm, tn = o_ref.shape

    @pl.when(pl.program_id(1) == 0)
    def _():
        for k0 in range(0, w_ref.shape[0], kchunk):
            wb_ref[k0:k0 + kchunk, :] = w_ref[k0:k0 + kchunk, :].astype(wb_ref.dtype)

    for r0 in range(0, tm, rm):
        rows = slice(r0, r0 + rm)
        a = jnp.dot(x_ref[rows, :], wb_ref[...], preferred_element_type=F32)
        if kind in _ROPE_SCALE:
            cos, sin = cos_ref[rows, :], sin_ref[rows, :]
            for c0 in range(0, tn, HEAD_W):
                o_ref[rows, c0:c0 + HEAD_W] = _rope(a[:, c0:c0 + HEAD_W], cos, sin,
                                                    _ROPE_SCALE[kind]).astype(o_ref.dtype)
        else:
            o_ref[rows, :] = _POINTWISE[kind](a).astype(o_ref.dtype)


def _cast_body(x_ref, o_ref):
    o_ref[...] = x_ref[...].astype(o_ref.dtype)


def _to_bf16(a, *, block_bytes=8 * 1024 * 1024):
    L, R, C = a.shape
    tr = R
    while tr * C * 4 > block_bytes and tr % 16 == 0:
        tr //= 2
    return pl.pallas_call(
        _cast_body,
        grid=(L, R // tr),
        in_specs=[pl.BlockSpec((None, tr, C), lambda l, i: (l, i, 0))],
        out_specs=pl.BlockSpec((None, tr, C), lambda l, i: (l, i, 0)),
        out_shape=jax.ShapeDtypeStruct(a.shape, BF16),
        compiler_params=_cparams("parallel", "parallel"),
        name="to_bf16",
    )(a)


def _in_projection(xb, w, layer, cos, sin, *, kind, col0, ncols, tm=1024, tn=512, rm=256):
    S, D = xb.shape
    tm = min(tm, S)
    j0 = col0 // tn
    return pl.pallas_call(
        functools.partial(_inproj_body, kind=kind, rm=rm, kchunk=512),
        grid=(ncols // tn, S // tm),
        in_specs=[pl.BlockSpec((tm, D), lambda j, i: (i, 0)),
                  pl.BlockSpec((None, D, tn), lambda j, i: (layer, 0, j0 + j)),
                  pl.BlockSpec((tm, HEAD_W), lambda j, i: (i, 0)),
                  pl.BlockSpec((tm, HEAD_W), lambda j, i: (i, 0))],
        out_specs=pl.BlockSpec((tm, tn), lambda j, i: (i, j)),
        out_shape=jax.ShapeDtypeStruct((S, ncols), BF16),
        scratch_shapes=[pltpu.VMEM((D, tn), BF16)],
        compiler_params=_cparams("arbitrary", "arbitrary"),
        name="in_projection_" + kind,
    )(xb, w, cos, sin)


FIRST = 2
V_ROWS = HEAD_W + 16


def _attn_body(lam_ref, g_ref, q_ref, qn_ref, k_ref, v_ref, gate_ref, o_ref,
               qt_ref, vt_ref, st_ref, bm_ref, pt_ref, al_ref, m_ref, acc_ref,
               *, tq, tk, rc, lambda_init):
    qi = pl.program_id(1)
    halves = range(2)
    n_ch = tk // rc
    kc = 2 * rc
    nd = tq // tk
    assert tk % kc == 0 and tq % tk == 0 and nd % 2 == 0

    def by_vreg(a):
        return a.reshape(a.shape[0] // SUBLANES, SUBLANES, a.shape[1])

    all_q = slice(0, tq)

    def score(b, slot, qs=all_q):
        k = k_ref[pl.ds(pl.multiple_of(b * tk, tk), tk), :]
        bm = []
        for c in halves:
            st = jnp.dot(k, qt_ref[c, :, qs], preferred_element_type=F32)
            st_ref[slot, c, :, qs] = st
            bm.append(jnp.max(by_vreg(st), axis=0))
        return bm

    def score_end(slot, bm, qs=all_q):
        for c in halves:
            bm_ref[slot, c, :, qs] = jnp.max(bm[c], axis=0, keepdims=True)

    def load_scores(slot, c, r, masked, qs):
        st = st_ref[slot, c, r * rc:(r + 1) * rc, qs]
        if masked is not None:
            k_id = lax.broadcasted_iota(jnp.int32, st.shape, 0) + (r * rc + masked)
            q_id = lax.broadcasted_iota(jnp.int32, st.shape, 1) + qs.start
            st = jnp.where(k_id <= q_id, st, MASK_VALUE)
        return st

    def prob_begin(slot, masked, qs):
        if masked is not None:
            bmax = []
            for c in halves:
                bm = jnp.max(by_vreg(load_scores(slot, c, 0, masked, qs)), axis=0)
                for r in range(1, n_ch):
                    bm = jnp.maximum(bm, jnp.max(by_vreg(load_scores(slot, c, r, masked, qs)),
                                                 axis=0))
                bmax.append(jnp.max(bm, axis=0, keepdims=True))
        else:
            bmax = [bm_ref[slot, c, :, qs] for c in halves]
        m_prev = [m_ref[c, :, qs] for c in halves]
        m_cur = [jnp.maximum(m_prev[c], bmax[c]) for c in halves]
        alpha = [jnp.exp2(m_prev[c] - m_cur[c]) for c in halves]
        return m_cur, alpha

    def prob_chunk(slot, par, r, masked, qs, m_cur):
        for c in halves:
            x = load_scores(slot, c, r, masked, qs) - m_cur[c]
            pt_ref[par, c, r * rc:(r + 1) * rc, qs] = jnp.exp2(x.astype(pt_ref.dtype))

    def prob_end(par, qs, m_cur, alpha):
        for c in halves:
            al_ref[par, c, :, qs] = alpha[c]
            m_ref[c, :, qs] = m_cur[c]

    def value_chunk(b, par, j, qs, part):
        keys = slice(j * kc, (j + 1) * kc)
        vt = vt_ref[b, :, keys]
        for c in halves:
            pv = jnp.dot(vt, pt_ref[par, c, keys, qs], preferred_element_type=F32)
            part[c] = pv if part[c] is None else part[c] + pv

    def value_end(par, qs, part):
        for c in halves:
            acc_ref[c, :, qs] = al_ref[par, c, :, qs] * acc_ref[c, :, qs] + part[c]

    def consume(b, slot, par, masked, qs=all_q, next_slot=None, next_qs=all_q):
        m_cur, alpha = prob_begin(slot, masked, qs)
        part = [None, None]
        bm = score(b + 1, next_slot, next_qs) if next_slot is not None else None
        for r in range(n_ch):
            prob_chunk(slot, par, r, masked, qs, m_cur)
            if (r + 1) * rc % kc == 0:
                value_chunk(b, par, (r + 1) * rc // kc - 1, qs, part)
        if bm is not None:
            score_end(next_slot, bm, next_qs)
        prob_end(par, qs, m_cur, alpha)
        value_end(par, qs, part)

    def start_block(ref):
        qt = ref[...].astype(F32).T
        row = lax.broadcasted_iota(jnp.int32, qt.shape, 0)
        qt_ref[0] = jnp.where(row < HEAD_DIM, qt, 0.0).astype(qt_ref.dtype)
        qt_ref[1] = jnp.where(row >= HEAD_DIM, qt, 0.0).astype(qt_ref.dtype)
        score_end(FIRST, score(0, FIRST))

    @pl.when(qi == 0)
    def _():
        ones = jnp.ones((V_ROWS - HEAD_W, tk), vt_ref.dtype)
        for kb in range(vt_ref.shape[0]):
            vt_ref[kb, 0:HEAD_W, :] = (v_ref[kb * tk:(kb + 1) * tk, :].astype(F32).T
                                       .astype(vt_ref.dtype))
            vt_ref[kb, HEAD_W:V_ROWS, :] = ones
        start_block(q_ref)

    m_ref[...] = jnp.full(m_ref.shape, MASK_VALUE, F32)
    acc_ref[...] = jnp.zeros(acc_ref.shape, F32)

    @pl.when(qi > 0)
    def _():
        consume(0, FIRST, 0, None, next_slot=1)

    def body(b, carry):
        lax.cond(b % 2 == 0,
                 lambda: consume(b, 0, 0, None, next_slot=1),
                 lambda: consume(b, 1, 1, None, next_slot=0))
        return carry

    lax.fori_loop(1, nd * qi, body, 0)

    def seen_by(d):
        return slice(d * tk, tq)

    for cond, slot in ((qi == 0, FIRST), (qi > 0, 0)):
        @pl.when(cond)
        def _():
            consume(nd * qi, slot, 0, 0, next_slot=1, next_qs=seen_by(1))

    for d in range(1, nd - 1):
        consume(nd * qi + d, d % 2, d % 2, d * tk, qs=seen_by(d),
                next_slot=(d + 1) % 2, next_qs=seen_by(d + 1))

    start_block(qn_ref)
    d = nd - 1
    consume(nd * qi + d, d % 2, d % 2, d * tk, qs=seen_by(d))

    lv = lam_ref[...]
    lam = (jnp.exp(jnp.sum(lv[0:1] * lv[1:2], axis=1, keepdims=True))
           - jnp.exp(jnp.sum(lv[2:3] * lv[3:4], axis=1, keepdims=True)) + lambda_init)
    num = [acc_ref[c, 0:HEAD_W, :] for c in halves]
    den = [acc_ref[c, HEAD_W:HEAD_W + 1, :] for c in halves]
    ot = num[0] * (1.0 / den[0]) - lam * (num[1] * (1.0 / den[1]))
    ot = ot * lax.rsqrt(jnp.mean(ot * ot, axis=0, keepdims=True) + RMS_EPS)
    ot = ot * g_ref[...] * (1.0 - lambda_init)
    o_ref[...] = (ot.T * gate_ref[...].astype(F32)).astype(o_ref.dtype)


def _attention(q, k, v, gate, lam_vecs, subln_g, *, lambda_init, tq=1024, tk=512):
    S = q.shape[0]
    tq = min(tq, S)
    tk = min(tk, tq // 2)
    n_q = S // tq
    H = q.shape[1] // HEAD_W
    return pl.pallas_call(
        functools.partial(_attn_body, tq=tq, tk=tk, rc=min(128, tk // 2),
                          lambda_init=lambda_init),
        grid=(H, n_q),
        in_specs=[pl.BlockSpec((4, HEAD_DIM), lambda hh, i: (0, 0)),
                  pl.BlockSpec((HEAD_W, 1), lambda hh, i: (0, 0)),
                  pl.BlockSpec((tq, HEAD_W), lambda hh, i: (i, hh)),
                  pl.BlockSpec((tq, HEAD_W),
                               lambda hh, i: (jnp.minimum(i + 1, n_q - 1), hh)),
                  pl.BlockSpec((S, HEAD_W), lambda hh, i: (0, hh)),
                  pl.BlockSpec((S, HEAD_W), lambda hh, i: (0, hh)),
                  pl.BlockSpec((tq, HEAD_W), lambda hh, i: (i, hh))],
        out_specs=pl.BlockSpec((tq, HEAD_W), lambda hh, i: (i, hh)),
        out_shape=jax.ShapeDtypeStruct((S, H * HEAD_W), BF16),
        scratch_shapes=[pltpu.VMEM((2, HEAD_W, tq), BF16),
                        pltpu.VMEM((S // tk, V_ROWS, tk), BF16),
                        pltpu.VMEM((3, 2, tk, tq), F32),
                        pltpu.VMEM((3, 2, 1, tq), F32),
                        pltpu.VMEM((2, 2, tk, tq), BF16),
                        pltpu.VMEM((2, 2, 1, tq), F32),
                        pltpu.VMEM((2, 1, tq), F32),
                        pltpu.VMEM((2, V_ROWS, tq), F32)],
        compiler_params=_cparams("arbitrary", "arbitrary"),
        name="diff_attention",
    )(lam_vecs, subln_g.reshape(HEAD_W, 1), q, q, k, v, gate)


def _conv_body(val_ref, glu_ref, pval_ref, pglu_ref, gate_ref, dww_ref, dwb_ref, lng_ref,
               lnb_ref, pww_ref, pwb_ref, o_ref, ext_ref, sh_ref, y_ref, *, rc, cc):
    i = pl.program_id(0)
    T, C = y_ref.shape
    L = T + CONV_HALO
    prev = pval_ref[...].astype(F32) * pglu_ref[...].astype(F32)
    ext_ref[0:CONV_HALO, :] = jnp.where(i > 0, prev, 0.0)
    ext_ref[CONV_HALO:L, :] = val_ref[...].astype(F32) * glu_ref[...].astype(F32)
    ext_ref[L:L + SUBLANES, :] = jnp.zeros((SUBLANES, C), F32)

    for s in range(1, SUBLANES):
        for c0 in range(0, C, cc):
            sh_ref[s - 1, :, c0:c0 + cc] = ext_ref[s:s + L, c0:c0 + cc]

    def window(off, r0, c0):
        s, whole = off % SUBLANES, off - off % SUBLANES
        rows = slice(r0 + whole, r0 + whole + rc)
        return ext_ref[rows, c0:c0 + cc] if s == 0 else sh_ref[s - 1, rows, c0:c0 + cc]

    base = CONV_HALO - (CONV_K - 1)
    for r0 in range(0, T, rc):
        for c0 in range(0, C, cc):
            acc = jnp.zeros((rc, cc), F32) + dwb_ref[:, c0:c0 + cc]
            for j in range(CONV_K):
                acc = acc + dww_ref[j:j + 1, c0:c0 + cc] * window(base + j, r0, c0)
            y_ref[r0:r0 + rc, c0:c0 + cc] = acc

    a = _silu(_layer_norm_rows(y_ref[...], lng_ref[...], lnb_ref[...]))
    z = jnp.dot(a.astype(BF16), pww_ref[...], preferred_element_type=F32) + pwb_ref[...]
    o_ref[...] = (z * gate_ref[...].astype(F32)).astype(o_ref.dtype)


def _conv_branch(val, glu, gate, dw_w, dw_b, ln_g, ln_b, pw_wb, layer, pw_b, *, T=256):
    S, C = val.shape
    T = min(T, S)
    hb = T // CONV_HALO
    row = lambda a: a.reshape(1, C)
    prev_map = lambda i: (jnp.maximum(i * hb - 1, 0), 0)
    return pl.pallas_call(
        functools.partial(_conv_body, rc=64, cc=256),
        grid=(S // T,),
        in_specs=[pl.BlockSpec((T, C), lambda i: (i, 0)),
                  pl.BlockSpec((T, C), lambda i: (i, 0)),
                  pl.BlockSpec((CONV_HALO, C), prev_map),
                  pl.BlockSpec((CONV_HALO, C), prev_map),
                  pl.BlockSpec((T, C), lambda i: (i, 0)),
                  pl.BlockSpec((CONV_K, C), lambda i: (0, 0)),
                  pl.BlockSpec((1, C), lambda i: (0, 0)),
                  pl.BlockSpec((1, C), lambda i: (0, 0)),
                  pl.BlockSpec((1, C), lambda i: (0, 0)),
                  pl.BlockSpec((None, C, C), lambda i: (layer, 0, 0)),
                  pl.BlockSpec((1, C), lambda i: (0, 0))],
        out_specs=pl.BlockSpec((T, C), lambda i: (i, 0)),
        out_shape=jax.ShapeDtypeStruct((S, C), BF16),
        scratch_shapes=[pltpu.VMEM((T + CONV_HALO + SUBLANES, C), F32),
                        pltpu.VMEM((SUBLANES - 1, T + CONV_HALO, C), F32),
                        pltpu.VMEM((T, C), F32)],
        compiler_params=_cparams("parallel"),
        name="conv_branch",
    )(val, glu, val, glu, gate, dw_w, row(dw_b), row(ln_g), row(ln_b), pw_wb, row(pw_b))


def _sgu_body(u_ref, v_ref, gate_ref, lng_ref, lnb_ref, w_ref, b_ref, o_ref, vn_ref):
    T, W = vn_ref.shape
    vn_ref[...] = _layer_norm_rows(v_ref[...].astype(F32), lng_ref[...], lnb_ref[...]).astype(BF16)
    r_id = lax.broadcasted_iota(jnp.int32, (CHUNK, CHUNK), 0)
    c_id = lax.broadcasted_iota(jnp.int32, (CHUNK, CHUNK), 1)
    causal = c_id <= r_id
    for g in range(W // SGU_GROUP_W):
        cols = slice(g * SGU_GROUP_W, (g + 1) * SGU_GROUP_W)
        wg = jnp.where(causal, w_ref[g], 0.0).astype(BF16)
        bias = b_ref[g]
        for n in range(T // CHUNK):
            rows = slice(n * CHUNK, (n + 1) * CHUNK)
            mixed = jnp.dot(wg, vn_ref[rows, cols], preferred_element_type=F32) + bias
            o_ref[rows, cols] = (u_ref[rows, cols].astype(F32) * mixed
                                 * gate_ref[rows, cols].astype(F32)).astype(o_ref.dtype)


def _sgu_branch(uv, gate, ln_g, ln_b, w_s, b_s, *, T=512):
    S = uv.shape[0]
    G = w_s.shape[0]
    W = G * SGU_GROUP_W
    T = min(T, S)
    return pl.pallas_call(
        _sgu_body,
        grid=(S // T,),
        in_specs=[pl.BlockSpec((T, W), lambda i: (i, 0)),
                  pl.BlockSpec((T, W), lambda i: (i, 1)),
                  pl.BlockSpec((T, W), lambda i: (i, 0)),
                  pl.BlockSpec((1, W), lambda i: (0, 0)),
                  pl.BlockSpec((1, W), lambda i: (0, 0)),
                  pl.BlockSpec((G, CHUNK, CHUNK), lambda i: (0, 0, 0)),
                  pl.BlockSpec((G, CHUNK, 1), lambda i: (0, 0, 0))],
        out_specs=pl.BlockSpec((T, W), lambda i: (i, 0)),
        out_shape=jax.ShapeDtypeStruct((S, W), BF16),
        scratch_shapes=[pltpu.VMEM((T, W), BF16)],
        compiler_params=_cparams("parallel"),
        name="sgu_branch",
    )(uv, uv, gate, ln_g.reshape(1, W), ln_b.reshape(1, W), w_s, b_s.reshape(G, CHUNK, 1))


def _outproj_body(a_ref, c_ref, s_ref, wa_ref, wc_ref, ws_ref, x_ref, o_ref, *, alpha):
    y = jnp.dot(a_ref[...], wa_ref[...], preferred_element_type=F32)
    y = y + jnp.dot(c_ref[...], wc_ref[...], preferred_element_type=F32)
    y = y + jnp.dot(s_ref[...], ws_ref[...], preferred_element_type=F32)
    o_ref[...] = alpha * x_ref[...] + y


def _out_projection(attn, conv, sgu, w, layer, x, *, alpha, tm=1024, tn=1024):
    S, D = x.shape
    tm = min(tm, S)
    wa, wc, ws = attn.shape[1], conv.shape[1], sgu.shape[1]
    return pl.pallas_call(
        functools.partial(_outproj_body, alpha=alpha),
        grid=(S // tm, D // tn),
        in_specs=[pl.BlockSpec((tm, wa), lambda i, j: (i, 0)),
                  pl.BlockSpec((tm, wc), lambda i, j: (i, 0)),
                  pl.BlockSpec((tm, ws), lambda i, j: (i, 0)),
                  pl.BlockSpec((None, wa, tn), lambda i, j: (layer, 0, j)),
                  pl.BlockSpec((None, wc, tn), lambda i, j: (layer, wa // wc, j)),
                  pl.BlockSpec((None, ws, tn), lambda i, j: (layer, (wa + wc) // ws, j)),
                  pl.BlockSpec((tm, tn), lambda i, j: (i, j))],
        out_specs=pl.BlockSpec((tm, tn), lambda i, j: (i, j)),
        out_shape=jax.ShapeDtypeStruct((S, D), F32),
        compiler_params=_cparams("parallel", "arbitrary"),
        name="out_projection",
    )(attn, conv, sgu, w, w, w, x)


def _postln_body(r_ref, g_ref, b_ref, *o_refs):
    y = _layer_norm_rows(r_ref[...], g_ref[...], b_ref[...])
    for o_ref in o_refs:
        o_ref[...] = y.astype(o_ref.dtype)


def _post_layer_norm(r, g, b, *, with_bf16, tr=256):
    S, D = r.shape
    tr = min(tr, S)
    dtypes = (F32, BF16) if with_bf16 else (F32,)
    return pl.pallas_call(
        _postln_body,
        grid=(S // tr,),
        in_specs=[pl.BlockSpec((tr, D), lambda i: (i, 0)),
                  pl.BlockSpec((1, D), lambda i: (0, 0)),
                  pl.BlockSpec((1, D), lambda i: (0, 0))],
        out_specs=[pl.BlockSpec((tr, D), lambda i: (i, 0)) for _ in dtypes],
        out_shape=[jax.ShapeDtypeStruct((S, D), dt) for dt in dtypes],
        compiler_params=_cparams("parallel"),
        name="post_layer_norm",
    )(r, g.reshape(1, D), b.reshape(1, D))


def kernel(x, positions, w_in, attn_lambda, attn_subln_g, conv_dw_w, conv_dw_b, conv_ln_g, conv_ln_b, conv_pw_w, conv_pw_b, sgu_ln_g, sgu_ln_b, sgu_w, sgu_b, w_out, post_ln_g, post_ln_b):
    B, S, D = x.shape
    depth = w_in.shape[0]
    assert B == 1 and D % 4 == 0
    u = D // 4
    alpha = (2 * depth) ** 0.25
    segments = (("q", 0, 2, "rope_q"), ("k", 2, 2, "rope_k"), ("v", 4, 2, "ident"),
                ("a_gate", 6, 2, "silu"), ("c_val", 8, 1, "ident"), ("c_glu", 9, 1, "sigmoid"),
                ("c_gate", 10, 1, "silu"), ("s_uv", 11, 2, "gelu"), ("s_gate", 13, 1, "silu"))

    cos, sin = _rope_tables(positions)
    xf = x.reshape(S, D)
    xb = _to_bf16(x)[0]
    w_out_b, pw_w_b = _to_bf16(w_out), _to_bf16(conv_pw_w)
    for layer in range(depth):
        lambda_init = 0.8 - 0.6 * math.exp(-0.3 * layer)
        h = {name: _in_projection(xb, w_in, layer, cos, sin, kind=kind, col0=first * u,
                                  ncols=n * u)
             for name, first, n, kind in segments}
        attn = _attention(h["q"], h["k"], h["v"], h["a_gate"], attn_lambda[layer],
                          attn_subln_g[layer], lambda_init=lambda_init)
        conv = _conv_branch(h["c_val"], h["c_glu"], h["c_gate"], conv_dw_w[layer],
                            conv_dw_b[layer], conv_ln_g[layer], conv_ln_b[layer],
                            pw_w_b, layer, conv_pw_b[layer])
        sgu = _sgu_branch(h["s_uv"], h["s_gate"], sgu_ln_g[layer], sgu_ln_b[layer],
                          sgu_w[layer], sgu_b[layer])
        r = _out_projection(attn, conv, sgu, w_out_b, layer, xf, alpha=alpha)
        outs = _post_layer_norm(r, post_ln_g[layer], post_ln_b[layer],
                                with_bf16=layer + 1 < depth)
        xf, xb = outs[0], outs[-1]
    return xf.reshape(B, S, D)
```

```python
import functools
import math

import jax
import jax.numpy as jnp
from jax import lax
from jax.experimental import pallas as pl
from jax.experimental.pallas import tpu as pltpu

F32 = jnp.float32
BF16 = jnp.bfloat16

HEAD_DIM = 64
HEAD_W = 2 * HEAD_DIM
CONV_K = 31
CONV_HALO = 32
CHUNK = 128
SGU_GROUP_W = 128
ROPE_THETA = 10000.0
LN_EPS = 1e-5
RMS_EPS = 1e-5
MASK_VALUE = -1e30
Q_SCALE = HEAD_DIM ** -0.5 * math.log2(math.e)
SUBLANES = 8
BF16_SUBLANES = 16
V7X_VMEM_BYTES = 64 * 1024 * 1024
VMEM_LIMIT = V7X_VMEM_BYTES - 8 * 1024 * 1024


def _cparams(*sem):
    return pltpu.CompilerParams(dimension_semantics=sem, vmem_limit_bytes=VMEM_LIMIT)


def _layer_norm_rows(x, g, b):
    mu = jnp.mean(x, axis=-1, keepdims=True)
    xc = x - mu
    var = jnp.mean(xc * xc, axis=-1, keepdims=True)
    return xc * lax.rsqrt(var + LN_EPS) * g + b


def _silu(x):
    return x * jax.nn.sigmoid(x)


def _gelu(x):
    return 0.5 * x * (1.0 + lax.erf(x * (1.0 / math.sqrt(2.0))))


def _rope_body(pos_ref, inv_ref, cos_ref, sin_ref):
    ang = pos_ref[...].astype(F32) * inv_ref[...]
    lane = lax.broadcasted_iota(jnp.int32, ang.shape, 1)
    first_half = (lane % HEAD_DIM) < (HEAD_DIM // 2)
    s = jnp.sin(ang)
    cos_ref[...] = jnp.cos(ang)
    sin_ref[...] = jnp.where(first_half, -s, s)


def _rope_tables(positions):
    S = positions.shape[-1]
    ts = min(S, 1024)
    inv = ROPE_THETA ** (-jnp.arange(0, HEAD_DIM, 2, dtype=F32) / HEAD_DIM)
    inv = jnp.tile(inv, HEAD_W // (HEAD_DIM // 2))[None, :]
    return pl.pallas_call(
        _rope_body,
        grid=(S // ts,),
        in_specs=[pl.BlockSpec((ts, 1), lambda i: (i, 0)),
                  pl.BlockSpec((1, HEAD_W), lambda i: (0, 0))],
        out_specs=[pl.BlockSpec((ts, HEAD_W), lambda i: (i, 0))] * 2,
        out_shape=[jax.ShapeDtypeStruct((S, HEAD_W), F32)] * 2,
        compiler_params=_cparams("parallel"),
        name="rope_tables",
    )(positions.reshape(S, 1), inv)


def _rope(t, cos, sin, scale):
    lane = lax.broadcasted_iota(jnp.int32, t.shape, 1)
    first_half = (lane % HEAD_DIM) < (HEAD_DIM // 2)
    rot = jnp.where(first_half,
                    pltpu.roll(t, HEAD_W - HEAD_DIM // 2, 1),
                    pltpu.roll(t, HEAD_DIM // 2, 1))
    return (t * cos + rot * sin) * scale


_POINTWISE = {"ident": lambda t: t, "silu": _silu, "sigmoid": jax.nn.sigmoid, "gelu": _gelu}
_ROPE_SCALE = {"rope_q": Q_SCALE, "rope_k": 1.0}


def _inproj_body(x_ref, w_ref, cos_ref, sin_ref, o_ref, wb_ref, *, kind, rm, kchunk):
    tm, tn = o_ref.shape

    @pl.when(pl.program_id(1) == 0)
    def _():
        for k0 in range(0, w_ref.shape[0], kchunk):
            wb_ref[k0:k0 + kchunk, :] = w_ref[k0:k0 + kchunk, :].astype(wb_ref.dtype)

    for r0 in range(0, tm, rm):
        rows = slice(r0, r0 + rm)
        a = jnp.dot(x_ref[rows, :], wb_ref[...], preferred_element_type=F32)
        if kind in _ROPE_SCALE:
            cos, sin = cos_ref[rows, :], sin_ref[rows, :]
            for c0 in range(0, tn, HEAD_W):
                o_ref[rows, c0:c0 + HEAD_W] = _rope(a[:, c0:c0 + HEAD_W], cos, sin,
                                                    _ROPE_SCALE[kind]).astype(o_ref.dtype)
        else:
            o_ref[rows, :] = _POINTWISE[kind](a).astype(o_ref.dtype)


def _cast_body(x_ref, o_ref):
    o_ref[...] = x_ref[...].astype(o_ref.dtype)


def _to_bf16(a, *, block_bytes=8 * 1024 * 1024):
    L, R, C = a.shape
    tr = R
    while tr * C * 4 > block_bytes and tr % 16 == 0:
        tr //= 2
    return pl.pallas_call(
        _cast_body,
        grid=(L, R // tr),
        in_specs=[pl.BlockSpec((None, tr, C), lambda l, i: (l, i, 0))],
        out_specs=pl.BlockSpec((None, tr, C), lambda l, i: (l, i, 0)),
        out_shape=jax.ShapeDtypeStruct(a.shape, BF16),
        compiler_params=_cparams("parallel", "parallel"),
        name="to_bf16",
    )(a)


def _in_projection(xb, w, layer, cos, sin, *, kind, col0, ncols, tm=1024, tn=512, rm=128):
    S, D = xb.shape
    tm = min(tm, S)
    j0 = col0 // tn
    return pl.pallas_call(
        functools.partial(_inproj_body, kind=kind, rm=rm, kchunk=512),
        grid=(ncols // tn, S // tm),
        in_specs=[pl.BlockSpec((tm, D), lambda j, i: (i, 0)),
                  pl.BlockSpec((None, D, tn), lambda j, i: (layer, 0, j0 + j)),
                  pl.BlockSpec((tm, HEAD_W), lambda j, i: (i, 0)),
                  pl.BlockSpec((tm, HEAD_W), lambda j, i: (i, 0))],
        out_specs=pl.BlockSpec((tm, tn), lambda j, i: (i, j)),
        out_shape=jax.ShapeDtypeStruct((S, ncols), BF16),
        scratch_shapes=[pltpu.VMEM((D, tn), BF16)],
        compiler_params=_cparams("arbitrary", "arbitrary"),
        name="in_projection_" + kind,
    )(xb, w, cos, sin)


FIRST = 2
V_ROWS = HEAD_W + BF16_SUBLANES


def _attn_body(lam_ref, g_ref, q_ref, qn_ref, k_ref, v_ref, gate_ref, o_ref,
               qt_ref, vt_ref, st_ref, bm_ref, pt_ref, al_ref, m_ref, acc_ref,
               *, tq, tk, rc, lambda_init):
    qi = pl.program_id(1)
    halves = range(2)
    n_ch = tk // rc
    kc = 2 * rc
    nd = tq // tk
    assert tk % kc == 0 and tq % tk == 0 and nd % 2 == 0

    def by_vreg(a):
        return a.reshape(a.shape[0] // SUBLANES, SUBLANES, a.shape[1])

    all_q = slice(0, tq)

    def score(b, slot, qs=all_q):
        k = k_ref[pl.ds(pl.multiple_of(b * tk, tk), tk), :]
        bm = []
        for c in halves:
            st = jnp.dot(k, qt_ref[c, :, qs], preferred_element_type=F32)
            st_ref[slot, c, :, qs] = st
            bm.append(jnp.max(by_vreg(st), axis=0))
        return bm

    def score_end(slot, bm, qs=all_q):
        for c in halves:
            bm_ref[slot, c, :, qs] = jnp.max(bm[c], axis=0, keepdims=True)

    def load_scores(slot, c, r, masked, qs):
        st = st_ref[slot, c, r * rc:(r + 1) * rc, qs]
        if masked is not None:
            k_id = lax.broadcasted_iota(jnp.int32, st.shape, 0) + (r * rc + masked)
            q_id = lax.broadcasted_iota(jnp.int32, st.shape, 1) + qs.start
            st = jnp.where(k_id <= q_id, st, MASK_VALUE)
        return st

    def prob_begin(slot, masked, qs):
        if masked is not None:
            bmax = []
            for c in halves:
                bm = jnp.max(by_vreg(load_scores(slot, c, 0, masked, qs)), axis=0)
                for r in range(1, n_ch):
                    bm = jnp.maximum(bm, jnp.max(by_vreg(load_scores(slot, c, r, masked, qs)),
                                                 axis=0))
                bmax.append(jnp.max(bm, axis=0, keepdims=True))
        else:
            bmax = [bm_ref[slot, c, :, qs] for c in halves]
        m_prev = [m_ref[c, :, qs] for c in halves]
        m_cur = [jnp.maximum(m_prev[c], bmax[c]) for c in halves]
        alpha = [jnp.exp2(m_prev[c] - m_cur[c]) for c in halves]
        return m_cur, alpha

    def prob_chunk(slot, par, r, masked, qs, m_cur):
        for c in halves:
            x = load_scores(slot, c, r, masked, qs) - m_cur[c]
            pt_ref[par, c, r * rc:(r + 1) * rc, qs] = jnp.exp2(x.astype(pt_ref.dtype))

    def prob_end(par, qs, m_cur, alpha):
        for c in halves:
            al_ref[par, c, :, qs] = alpha[c]
            m_ref[c, :, qs] = m_cur[c]

    def value_chunk(b, par, j, qs, part):
        keys = slice(j * kc, (j + 1) * kc)
        vt = vt_ref[b, :, keys]
        for c in halves:
            pv = jnp.dot(vt, pt_ref[par, c, keys, qs], preferred_element_type=F32)
            part[c] = pv if part[c] is None else part[c] + pv

    def value_end(par, qs, part):
        for c in halves:
            acc_ref[c, :, qs] = al_ref[par, c, :, qs] * acc_ref[c, :, qs] + part[c]

    def consume(b, slot, par, masked, qs=all_q, next_slot=None, next_qs=all_q):
        m_cur, alpha = prob_begin(slot, masked, qs)
        part = [None, None]
        bm = score(b + 1, next_slot, next_qs) if next_slot is not None else None
        for r in range(n_ch):
            prob_chunk(slot, par, r, masked, qs, m_cur)
            if (r + 1) * rc % kc == 0:
                value_chunk(b, par, (r + 1) * rc // kc - 1, qs, part)
        if bm is not None:
            score_end(next_slot, bm, next_qs)
        prob_end(par, qs, m_cur, alpha)
        value_end(par, qs, part)

    def start_block(ref):
        qt = ref[...].astype(F32).T
        row = lax.broadcasted_iota(jnp.int32, qt.shape, 0)
        qt_ref[0] = jnp.where(row < HEAD_DIM, qt, 0.0).astype(qt_ref.dtype)
        qt_ref[1] = jnp.where(row >= HEAD_DIM, qt, 0.0).astype(qt_ref.dtype)
        score_end(FIRST, score(0, FIRST))

    @pl.when(qi == 0)
    def _():
        ones = jnp.ones((V_ROWS - HEAD_W, tk), vt_ref.dtype)
        for kb in range(vt_ref.shape[0]):
            vt_ref[kb, 0:HEAD_W, :] = (v_ref[kb * tk:(kb + 1) * tk, :].astype(F32).T
                                       .astype(vt_ref.dtype))
            vt_ref[kb, HEAD_W:V_ROWS, :] = ones
        start_block(q_ref)

    m_ref[...] = jnp.full(m_ref.shape, MASK_VALUE, F32)
    acc_ref[...] = jnp.zeros(acc_ref.shape, F32)

    @pl.when(qi > 0)
    def _():
        consume(0, FIRST, 0, None, next_slot=1)

    def body(b, carry):
        lax.cond(b % 2 == 0,
                 lambda: consume(b, 0, 0, None, next_slot=1),
                 lambda: consume(b, 1, 1, None, next_slot=0))
        return carry

    lax.fori_loop(1, nd * qi, body, 0)

    def seen_by(d):
        return slice(d * tk, tq)

    def diagonal(d, slot, with_next):
        b, par = nd * qi + d, d % 2
        nxt = dict(next_slot=(d + 1) % 2, next_qs=seen_by(d + 1)) if with_next else {}
        consume(b, slot, par, d * tk, qs=slice(d * tk, (d + 1) * tk), **nxt)
        if d + 1 < nd:
            consume(b, slot, par, None, qs=seen_by(d + 1))

    for cond, slot in ((qi == 0, FIRST), (qi > 0, 0)):
        @pl.when(cond)
        def _():
            diagonal(0, slot, True)

    for d in range(1, nd - 1):
        diagonal(d, d % 2, True)

    start_block(qn_ref)
    diagonal(nd - 1, (nd - 1) % 2, False)

    lv = lam_ref[...]
    lam = (jnp.exp(jnp.sum(lv[0:1] * lv[1:2], axis=1, keepdims=True))
           - jnp.exp(jnp.sum(lv[2:3] * lv[3:4], axis=1, keepdims=True)) + lambda_init)
    num = [acc_ref[c, 0:HEAD_W, :] for c in halves]
    den = [acc_ref[c, HEAD_W:HEAD_W + 1, :] for c in halves]
    ot = num[0] * (1.0 / den[0]) - lam * (num[1] * (1.0 / den[1]))
    ot = ot * lax.rsqrt(jnp.mean(ot * ot, axis=0, keepdims=True) + RMS_EPS)
    ot = ot * g_ref[...] * (1.0 - lambda_init)
    o_ref[...] = (ot.T * gate_ref[...].astype(F32)).astype(o_ref.dtype)


def _attention(q, k, v, gate, lam_vecs, subln_g, *, lambda_init, tq=1024, tk=512):
    S = q.shape[0]
    tq = min(tq, S)
    tk = min(tk, tq // 2)
    n_q = S // tq
    H = q.shape[1] // HEAD_W
    return pl.pallas_call(
        functools.partial(_attn_body, tq=tq, tk=tk, rc=min(128, tk // 2),
                          lambda_init=lambda_init),
        grid=(H, n_q),
        in_specs=[pl.BlockSpec((4, HEAD_DIM), lambda hh, i: (0, 0)),
                  pl.BlockSpec((HEAD_W, 1), lambda hh, i: (0, 0)),
                  pl.BlockSpec((tq, HEAD_W), lambda hh, i: (i, hh)),
                  pl.BlockSpec((tq, HEAD_W),
                               lambda hh, i: (jnp.minimum(i + 1, n_q - 1), hh)),
                  pl.BlockSpec((S, HEAD_W), lambda hh, i: (0, hh)),
                  pl.BlockSpec((S, HEAD_W), lambda hh, i: (0, hh)),
                  pl.BlockSpec((tq, HEAD_W), lambda hh, i: (i, hh))],
        out_specs=pl.BlockSpec((tq, HEAD_W), lambda hh, i: (i, hh)),
        out_shape=jax.ShapeDtypeStruct((S, H * HEAD_W), BF16),
        scratch_shapes=[pltpu.VMEM((2, HEAD_W, tq), BF16),
                        pltpu.VMEM((S // tk, V_ROWS, tk), BF16),
                        pltpu.VMEM((3, 2, tk, tq), F32),
                        pltpu.VMEM((3, 2, 1, tq), F32),
                        pltpu.VMEM((2, 2, tk, tq), BF16),
                        pltpu.VMEM((2, 2, 1, tq), F32),
                        pltpu.VMEM((2, 1, tq), F32),
                        pltpu.VMEM((2, V_ROWS, tq), F32)],
        compiler_params=_cparams("arbitrary", "arbitrary"),
        name="diff_attention",
    )(lam_vecs, subln_g.reshape(HEAD_W, 1), q, q, k, v, gate)


def _conv_body(val_ref, glu_ref, pval_ref, pglu_ref, gate_ref, dww_ref, dwb_ref, lng_ref,
               lnb_ref, pww_ref, pwb_ref, o_ref, ext_ref, sh_ref, y_ref, *, rc, cc):
    i = pl.program_id(0)
    T, C = y_ref.shape
    L = T + CONV_HALO
    prev = pval_ref[...].astype(F32) * pglu_ref[...].astype(F32)
    ext_ref[0:CONV_HALO, :] = jnp.where(i > 0, prev, 0.0)
    ext_ref[CONV_HALO:L, :] = val_ref[...].astype(F32) * glu_ref[...].astype(F32)
    ext_ref[L:L + SUBLANES, :] = jnp.zeros((SUBLANES, C), F32)

    for s in range(1, SUBLANES):
        for c0 in range(0, C, cc):
            sh_ref[s - 1, :, c0:c0 + cc] = ext_ref[s:s + L, c0:c0 + cc]

    def window(off, r0, c0):
        s, whole = off % SUBLANES, off - off % SUBLANES
        rows = slice(r0 + whole, r0 + whole + rc)
        return ext_ref[rows, c0:c0 + cc] if s == 0 else sh_ref[s - 1, rows, c0:c0 + cc]

    base = CONV_HALO - (CONV_K - 1)
    for r0 in range(0, T, rc):
        for c0 in range(0, C, cc):
            acc = jnp.zeros((rc, cc), F32) + dwb_ref[:, c0:c0 + cc]
            for j in range(CONV_K):
                acc = acc + dww_ref[j:j + 1, c0:c0 + cc] * window(base + j, r0, c0)
            y_ref[r0:r0 + rc, c0:c0 + cc] = acc

    a = _silu(_layer_norm_rows(y_ref[...], lng_ref[...], lnb_ref[...]))
    z = jnp.dot(a.astype(BF16), pww_ref[...], preferred_element_type=F32) + pwb_ref[...]
    o_ref[...] = (z * gate_ref[...].astype(F32)).astype(o_ref.dtype)


def _conv_branch(val, glu, gate, dw_w, dw_b, ln_g, ln_b, pw_wb, layer, pw_b, *, T=256):
    S, C = val.shape
    T = min(T, S)
    hb = T // CONV_HALO
    row = lambda a: a.reshape(1, C)
    prev_map = lambda i: (jnp.maximum(i * hb - 1, 0), 0)
    return pl.pallas_call(
        functools.partial(_conv_body, rc=64, cc=256),
        grid=(S // T,),
        in_specs=[pl.BlockSpec((T, C), lambda i: (i, 0)),
                  pl.BlockSpec((T, C), lambda i: (i, 0)),
                  pl.BlockSpec((CONV_HALO, C), prev_map),
                  pl.BlockSpec((CONV_HALO, C), prev_map),
                  pl.BlockSpec((T, C), lambda i: (i, 0)),
                  pl.BlockSpec((CONV_K, C), lambda i: (0, 0)),
                  pl.BlockSpec((1, C), lambda i: (0, 0)),
                  pl.BlockSpec((1, C), lambda i: (0, 0)),
                  pl.BlockSpec((1, C), lambda i: (0, 0)),
                  pl.BlockSpec((None, C, C), lambda i: (layer, 0, 0)),
                  pl.BlockSpec((1, C), lambda i: (0, 0))],
        out_specs=pl.BlockSpec((T, C), lambda i: (i, 0)),
        out_shape=jax.ShapeDtypeStruct((S, C), BF16),
        scratch_shapes=[pltpu.VMEM((T + CONV_HALO + SUBLANES, C), F32),
                        pltpu.VMEM((SUBLANES - 1, T + CONV_HALO, C), F32),
                        pltpu.VMEM((T, C), F32)],
        compiler_params=_cparams("parallel"),
        name="conv_branch",
    )(val, glu, val, glu, gate, dw_w, row(dw_b), row(ln_g), row(ln_b), pw_wb, row(pw_b))


def _sgu_body(u_ref, v_ref, gate_ref, lng_ref, lnb_ref, w_ref, b_ref, o_ref, vn_ref):
    T, W = vn_ref.shape
    vn_ref[...] = _layer_norm_rows(v_ref[...].astype(F32), lng_ref[...], lnb_ref[...]).astype(BF16)
    r_id = lax.broadcasted_iota(jnp.int32, (CHUNK, CHUNK), 0)
    c_id = lax.broadcasted_iota(jnp.int32, (CHUNK, CHUNK), 1)
    causal = c_id <= r_id
    for g in range(W // SGU_GROUP_W):
        cols = slice(g * SGU_GROUP_W, (g + 1) * SGU_GROUP_W)
        wg = jnp.where(causal, w_ref[g], 0.0).astype(BF16)
        bias = b_ref[g]
        for n in range(T // CHUNK):
            rows = slice(n * CHUNK, (n + 1) * CHUNK)
            mixed = jnp.dot(wg, vn_ref[rows, cols], preferred_element_type=F32) + bias
            o_ref[rows, cols] = (u_ref[rows, cols].astype(F32) * mixed
                                 * gate_ref[rows, cols].astype(F32)).astype(o_ref.dtype)


def _sgu_branch(uv, gate, ln_g, ln_b, w_s, b_s, *, T=512):
    S = uv.shape[0]
    G = w_s.shape[0]
    W = G * SGU_GROUP_W
    T = min(T, S)
    return pl.pallas_call(
        _sgu_body,
        grid=(S // T,),
        in_specs=[pl.BlockSpec((T, W), lambda i: (i, 0)),
                  pl.BlockSpec((T, W), lambda i: (i, 1)),
                  pl.BlockSpec((T, W), lambda i: (i, 0)),
                  pl.BlockSpec((1, W), lambda i: (0, 0)),
                  pl.BlockSpec((1, W), lambda i: (0, 0)),
                  pl.BlockSpec((G, CHUNK, CHUNK), lambda i: (0, 0, 0)),
                  pl.BlockSpec((G, CHUNK, 1), lambda i: (0, 0, 0))],
        out_specs=pl.BlockSpec((T, W), lambda i: (i, 0)),
        out_shape=jax.ShapeDtypeStruct((S, W), BF16),
        scratch_shapes=[pltpu.VMEM((T, W), BF16)],
        compiler_params=_cparams("parallel"),
        name="sgu_branch",
    )(uv, uv, gate, ln_g.reshape(1, W), ln_b.reshape(1, W), w_s, b_s.reshape(G, CHUNK, 1))


def _outproj_body(a_ref, c_ref, s_ref, wa_ref, wc_ref, ws_ref, x_ref, o_ref, *, alpha):
    y = jnp.dot(a_ref[...], wa_ref[...], preferred_element_type=F32)
    y = y + jnp.dot(c_ref[...], wc_ref[...], preferred_element_type=F32)
    y = y + jnp.dot(s_ref[...], ws_ref[...], preferred_element_type=F32)
    o_ref[...] = alpha * x_ref[...] + y


def _out_projection(attn, conv, sgu, w, layer, x, *, alpha, tm=1024, tn=1024):
    S, D = x.shape
    tm = min(tm, S)
    wa, wc, ws = attn.shape[1], conv.shape[1], sgu.shape[1]
    return pl.pallas_call(
        functools.partial(_outproj_body, alpha=alpha),
        grid=(S // tm, D // tn),
        in_specs=[pl.BlockSpec((tm, wa), lambda i, j: (i, 0)),
                  pl.BlockSpec((tm, wc), lambda i, j: (i, 0)),
                  pl.BlockSpec((tm, ws), lambda i, j: (i, 0)),
                  pl.BlockSpec((None, wa, tn), lambda i, j: (layer, 0, j)),
                  pl.BlockSpec((None, wc, tn), lambda i, j: (layer, wa // wc, j)),
                  pl.BlockSpec((None, ws, tn), lambda i, j: (layer, (wa + wc) // ws, j)),
                  pl.BlockSpec((tm, tn), lambda i, j: (i, j))],
        out_specs=pl.BlockSpec((tm, tn), lambda i, j: (i, j)),
        out_shape=jax.ShapeDtypeStruct((S, D), F32),
        compiler_params=_cparams("parallel", "arbitrary"),
        name="out_projection",
    )(attn, conv, sgu, w, w, w, x)


def _postln_body(r_ref, g_ref, b_ref, *o_refs):
    y = _layer_norm_rows(r_ref[...], g_ref[...], b_ref[...])
    for o_ref in o_refs:
        o_ref[...] = y.astype(o_ref.dtype)


def _post_layer_norm(r, g, b, *, with_bf16, tr=256):
    S, D = r.shape
    tr = min(tr, S)
    dtypes = (F32, BF16) if with_bf16 else (F32,)
    return pl.pallas_call(
        _postln_body,
        grid=(S // tr,),
        in_specs=[pl.BlockSpec((tr, D), lambda i: (i, 0)),
                  pl.BlockSpec((1, D), lambda i: (0, 0)),
                  pl.BlockSpec((1, D), lambda i: (0, 0))],
        out_specs=[pl.BlockSpec((tr, D), lambda i: (i, 0)) for _ in dtypes],
        out_shape=[jax.ShapeDtypeStruct((S, D), dt) for dt in dtypes],
        compiler_params=_cparams("parallel"),
        name="post_layer_norm",
    )(r, g.reshape(1, D), b.reshape(1, D))


def kernel(x, positions, w_in, attn_lambda, attn_subln_g, conv_dw_w, conv_dw_b, conv_ln_g, conv_ln_b, conv_pw_w, conv_pw_b, sgu_ln_g, sgu_ln_b, sgu_w, sgu_b, w_out, post_ln_g, post_ln_b):
    B, S, D = x.shape
    depth = w_in.shape[0]
    assert B == 1 and S % 1024 == 0 and D % 2048 == 0
    u = D // 4
    alpha = (2 * depth) ** 0.25
    segments = (("q", 0, 2, "rope_q"), ("k", 2, 2, "rope_k"), ("v", 4, 2, "ident"),
                ("a_gate", 6, 2, "silu"), ("c_val", 8, 1, "ident"), ("c_glu", 9, 1, "sigmoid"),
                ("c_gate", 10, 1, "silu"), ("s_uv", 11, 2, "gelu"), ("s_gate", 13, 1, "silu"))

    cos, sin = _rope_tables(positions)
    xf = x.reshape(S, D)
    xb = _to_bf16(x)[0]
    w_out_b, pw_w_b = _to_bf16(w_out), _to_bf16(conv_pw_w)
    for layer in range(depth):
        lambda_init = 0.8 - 0.6 * math.exp(-0.3 * layer)
        h = {name: _in_projection(xb, w_in, layer, cos, sin, kind=kind, col0=first * u,
                                  ncols=n * u)
             for name, first, n, kind in segments}
        attn = _attention(h["q"], h["k"], h["v"], h["a_gate"], attn_lambda[layer],
                          attn_subln_g[layer], lambda_init=lambda_init)
        conv = _conv_branch(h["c_val"], h["c_glu"], h["c_gate"], conv_dw_w[layer],
                            conv_dw_b[layer], conv_ln_g[layer], conv_ln_b[layer],
                            pw_w_b, layer, conv_pw_b[layer])
        sgu = _sgu_branch(h["s_uv"], h["s_gate"], sgu_ln_g[layer], sgu_ln_b[layer],
                          sgu_w[layer], sgu_b[layer])
        r = _out_projection(attn, conv, sgu, w_out_b, layer, xf, alpha=alpha)
        outs = _post_layer_norm(r, post_ln_g[layer], post_ln_b[layer],
                                with_bf16=layer + 1 < depth)
        xf, xb = outs[0], outs[-1]
    return xf.reshape(B, S, D)
```

```python
import functools
import math

import jax
import jax.numpy as jnp
from jax import lax
from jax.experimental import pallas as pl
from jax.experimental.pallas import tpu as pltpu

F32 = jnp.float32
BF16 = jnp.bfloat16

HEAD_DIM = 64
HEAD_W = 2 * HEAD_DIM
CONV_K = 31
CONV_HALO = 32
CHUNK = 128
SGU_GROUP_W = 128
ROPE_THETA = 10000.0
LN_EPS = 1e-5
RMS_EPS = 1e-5
MASK_VALUE = -1e30
Q_SCALE = HEAD_DIM ** -0.5 * math.log2(math.e)
SUBLANES = 8
BF16_SUBLANES = 16
V7X_VMEM_BYTES = 64 * 1024 * 1024
VMEM_LIMIT = V7X_VMEM_BYTES - 8 * 1024 * 1024


def _cparams(*sem):
    return pltpu.CompilerParams(dimension_semantics=sem, vmem_limit_bytes=VMEM_LIMIT)


def _layer_norm_rows(x, g, b):
    mu = jnp.mean(x, axis=-1, keepdims=True)
    xc = x - mu
    var = jnp.mean(xc * xc, axis=-1, keepdims=True)
    return xc * lax.rsqrt(var + LN_EPS) * g + b


def _silu(x):
    return x * jax.nn.sigmoid(x)


def _gelu(x):
    return 0.5 * x * (1.0 + lax.erf(x * (1.0 / math.sqrt(2.0))))


def _rope_body(pos_ref, inv_ref, cos_ref, sin_ref):
    ang = pos_ref[...].astype(F32) * inv_ref[...]
    lane = lax.broadcasted_iota(jnp.int32, ang.shape, 1)
    first_half = (lane % HEAD_DIM) < (HEAD_DIM // 2)
    s = jnp.sin(ang)
    cos_ref[...] = jnp.cos(ang)
    sin_ref[...] = jnp.where(first_half, -s, s)


def _rope_tables(positions):
    S = positions.shape[-1]
    ts = min(S, 1024)
    inv = ROPE_THETA ** (-jnp.arange(0, HEAD_DIM, 2, dtype=F32) / HEAD_DIM)
    inv = jnp.tile(inv, HEAD_W // (HEAD_DIM // 2))[None, :]
    return pl.pallas_call(
        _rope_body,
        grid=(S // ts,),
        in_specs=[pl.BlockSpec((ts, 1), lambda i: (i, 0)),
                  pl.BlockSpec((1, HEAD_W), lambda i: (0, 0))],
        out_specs=[pl.BlockSpec((ts, HEAD_W), lambda i: (i, 0))] * 2,
        out_shape=[jax.ShapeDtypeStruct((S, HEAD_W), F32)] * 2,
        compiler_params=_cparams("parallel"),
        name="rope_tables",
    )(positions.reshape(S, 1), inv)


def _rope(t, cos, sin, scale):
    lane = lax.broadcasted_iota(jnp.int32, t.shape, 1)
    first_half = (lane % HEAD_DIM) < (HEAD_DIM // 2)
    rot = jnp.where(first_half,
                    pltpu.roll(t, HEAD_W - HEAD_DIM // 2, 1),
                    pltpu.roll(t, HEAD_DIM // 2, 1))
    return (t * cos + rot * sin) * scale


_POINTWISE = {"ident": lambda t: t, "silu": _silu, "sigmoid": jax.nn.sigmoid, "gelu": _gelu}
_ROPE_SCALE = {"rope_q": Q_SCALE, "rope_k": 1.0}


def _inproj_body(x_ref, w_ref, cos_ref, sin_ref, o_ref, wb_ref, *, kind, rm, kchunk):
    tm, tn = o_ref.shape

    @pl.when(pl.program_id(1) == 0)
    def _():
        for k0 in range(0, w_ref.shape[0], kchunk):
            wb_ref[k0:k0 + kchunk, :] = w_ref[k0:k0 + kchunk, :].astype(wb_ref.dtype)

    for r0 in range(0, tm, rm):
        rows = slice(r0, r0 + rm)
        a = jnp.dot(x_ref[rows, :], wb_ref[...], preferred_element_type=F32)
        if kind in _ROPE_SCALE:
            cos, sin = cos_ref[rows, :], sin_ref[rows, :]
            for c0 in range(0, tn, HEAD_W):
                o_ref[rows, c0:c0 + HEAD_W] = _rope(a[:, c0:c0 + HEAD_W], cos, sin,
                                                    _ROPE_SCALE[kind]).astype(o_ref.dtype)
        else:
            o_ref[rows, :] = _POINTWISE[kind](a).astype(o_ref.dtype)


def _cast_body(x_ref, o_ref):
    o_ref[...] = x_ref[...].astype(o_ref.dtype)


def _to_bf16(a, *, block_bytes=8 * 1024 * 1024):
    L, R, C = a.shape
    tr = R
    while tr * C * 4 > block_bytes and tr % 16 == 0:
        tr //= 2
    return pl.pallas_call(
        _cast_body,
        grid=(L, R // tr),
        in_specs=[pl.BlockSpec((None, tr, C), lambda l, i: (l, i, 0))],
        out_specs=pl.BlockSpec((None, tr, C), lambda l, i: (l, i, 0)),
        out_shape=jax.ShapeDtypeStruct(a.shape, BF16),
        compiler_params=_cparams("parallel", "parallel"),
        name="to_bf16",
    )(a)


def _in_projection(xb, w, layer, cos, sin, *, kind, units, unit_w, tm=1024, tn=512, rm=128):
    S, D = xb.shape
    tm = min(tm, S)
    per_unit = unit_w // tn
    ncols = len(units) * unit_w

    def w_tile(j):
        first = jnp.int32(units[0] * per_unit)
        for n, unit in enumerate(units[1:], 1):
            first = jnp.where(j // per_unit == n, unit * per_unit, first)
        return first + j % per_unit

    return pl.pallas_call(
        functools.partial(_inproj_body, kind=kind, rm=rm, kchunk=512),
        grid=(ncols // tn, S // tm),
        in_specs=[pl.BlockSpec((tm, D), lambda j, i: (i, 0)),
                  pl.BlockSpec((None, D, tn), lambda j, i: (layer, 0, w_tile(j))),
                  pl.BlockSpec((tm, HEAD_W), lambda j, i: (i, 0)),
                  pl.BlockSpec((tm, HEAD_W), lambda j, i: (i, 0))],
        out_specs=pl.BlockSpec((tm, tn), lambda j, i: (i, j)),
        out_shape=jax.ShapeDtypeStruct((S, ncols), BF16),
        scratch_shapes=[pltpu.VMEM((D, tn), BF16)],
        compiler_params=_cparams("arbitrary", "arbitrary"),
        name="in_projection_" + kind,
    )(xb, w, cos, sin)


FIRST = 2
V_ROWS = HEAD_W + BF16_SUBLANES


def _attn_body(lam_ref, g_ref, q_ref, qn_ref, k_ref, v_ref, gate_ref, o_ref,
               qt_ref, vt_ref, st_ref, bm_ref, pt_ref, al_ref, m_ref, acc_ref,
               *, tq, tk, rc, lambda_init):
    qi = pl.program_id(1)
    halves = range(2)
    n_ch = tk // rc
    kc = 2 * rc
    nd = tq // tk
    assert tk % kc == 0 and tq % tk == 0 and nd % 2 == 0

    def by_vreg(a):
        return a.reshape(a.shape[0] // SUBLANES, SUBLANES, a.shape[1])

    all_q = slice(0, tq)

    def score(b, slot, qs=all_q):
        k = k_ref[pl.ds(pl.multiple_of(b * tk, tk), tk), :]
        bm = []
        for c in halves:
            st = jnp.dot(k, qt_ref[c, :, qs], preferred_element_type=F32)
            st_ref[slot, c, :, qs] = st
            bm.append(jnp.max(by_vreg(st), axis=0))
        return bm

    def score_end(slot, bm, qs=all_q):
        for c in halves:
            bm_ref[slot, c, :, qs] = jnp.max(bm[c], axis=0, keepdims=True)

    def load_scores(slot, c, r, masked, qs):
        st = st_ref[slot, c, r * rc:(r + 1) * rc, qs]
        if masked is not None:
            k_id = lax.broadcasted_iota(jnp.int32, st.shape, 0) + (r * rc + masked)
            q_id = lax.broadcasted_iota(jnp.int32, st.shape, 1) + qs.start
            st = jnp.where(k_id <= q_id, st, MASK_VALUE)
        return st

    def prob_begin(slot, masked, qs):
        if masked is not None:
            bmax = []
            for c in halves:
                bm = jnp.max(by_vreg(load_scores(slot, c, 0, masked, qs)), axis=0)
                for r in range(1, n_ch):
                    bm = jnp.maximum(bm, jnp.max(by_vreg(load_scores(slot, c, r, masked, qs)),
                                                 axis=0))
                bmax.append(jnp.max(bm, axis=0, keepdims=True))
        else:
            bmax = [bm_ref[slot, c, :, qs] for c in halves]
        m_prev = [m_ref[c, :, qs] for c in halves]
        m_cur = [jnp.maximum(m_prev[c], bmax[c]) for c in halves]
        alpha = [jnp.exp2(m_prev[c] - m_cur[c]) for c in halves]
        return m_cur, alpha

    def prob_chunk(slot, par, r, masked, qs, m_cur):
        for c in halves:
            x = load_scores(slot, c, r, masked, qs) - m_cur[c]
            pt_ref[par, c, r * rc:(r + 1) * rc, qs] = jnp.exp2(x.astype(pt_ref.dtype))

    def prob_end(par, qs, m_cur, alpha):
        for c in halves:
            al_ref[par, c, :, qs] = alpha[c]
            m_ref[c, :, qs] = m_cur[c]

    def value_chunk(b, par, j, qs, part):
        keys = slice(j * kc, (j + 1) * kc)
        vt = vt_ref[b, :, keys]
        for c in halves:
            pv = jnp.dot(vt, pt_ref[par, c, keys, qs], preferred_element_type=F32)
            part[c] = pv if part[c] is None else part[c] + pv

    def value_end(par, qs, part):
        for c in halves:
            acc_ref[c, :, qs] = al_ref[par, c, :, qs] * acc_ref[c, :, qs] + part[c]

    def consume(b, slot, par, masked, qs=all_q, next_slot=None, next_qs=all_q):
        m_cur, alpha = prob_begin(slot, masked, qs)
        part = [None, None]
        bm = score(b + 1, next_slot, next_qs) if next_slot is not None else None
        for r in range(n_ch):
            prob_chunk(slot, par, r, masked, qs, m_cur)
            if (r + 1) * rc % kc == 0:
                value_chunk(b, par, (r + 1) * rc // kc - 1, qs, part)
        if bm is not None:
            score_end(next_slot, bm, next_qs)
        prob_end(par, qs, m_cur, alpha)
        value_end(par, qs, part)

    def start_block(ref):
        qt = ref[...].astype(F32).T
        row = lax.broadcasted_iota(jnp.int32, qt.shape, 0)
        qt_ref[0] = jnp.where(row < HEAD_DIM, qt, 0.0).astype(qt_ref.dtype)
        qt_ref[1] = jnp.where(row >= HEAD_DIM, qt, 0.0).astype(qt_ref.dtype)
        score_end(FIRST, score(0, FIRST))

    @pl.when(qi == 0)
    def _():
        ones = jnp.ones((V_ROWS - HEAD_W, tk), vt_ref.dtype)
        for kb in range(vt_ref.shape[0]):
            vt_ref[kb, 0:HEAD_W, :] = (v_ref[kb * tk:(kb + 1) * tk, :].astype(F32).T
                                       .astype(vt_ref.dtype))
            vt_ref[kb, HEAD_W:V_ROWS, :] = ones
        start_block(q_ref)

    m_ref[...] = jnp.full(m_ref.shape, MASK_VALUE, F32)
    acc_ref[...] = jnp.zeros(acc_ref.shape, F32)

    @pl.when(qi > 0)
    def _():
        consume(0, FIRST, 0, None, next_slot=1)

    def body(b, carry):
        lax.cond(b % 2 == 0,
                 lambda: consume(b, 0, 0, None, next_slot=1),
                 lambda: consume(b, 1, 1, None, next_slot=0))
        return carry

    lax.fori_loop(1, nd * qi, body, 0)

    def seen_by(d):
        return slice(d * tk, tq)

    def diagonal(d, slot, with_next):
        b, par = nd * qi + d, d % 2
        nxt = dict(next_slot=(d + 1) % 2, next_qs=seen_by(d + 1)) if with_next else {}
        consume(b, slot, par, d * tk, qs=slice(d * tk, (d + 1) * tk), **nxt)
        if d + 1 < nd:
            consume(b, slot, par, None, qs=seen_by(d + 1))

    for cond, slot in ((qi == 0, FIRST), (qi > 0, 0)):
        @pl.when(cond)
        def _():
            diagonal(0, slot, True)

    for d in range(1, nd - 1):
        diagonal(d, d % 2, True)

    start_block(qn_ref)
    diagonal(nd - 1, (nd - 1) % 2, False)

    lv = lam_ref[...]
    lam = (jnp.exp(jnp.sum(lv[0:1] * lv[1:2], axis=1, keepdims=True))
           - jnp.exp(jnp.sum(lv[2:3] * lv[3:4], axis=1, keepdims=True)) + lambda_init)
    num = [acc_ref[c, 0:HEAD_W, :] for c in halves]
    den = [acc_ref[c, HEAD_W:HEAD_W + 1, :] for c in halves]
    ot = num[0] * (1.0 / den[0]) - lam * (num[1] * (1.0 / den[1]))
    ot = ot * lax.rsqrt(jnp.mean(ot * ot, axis=0, keepdims=True) + RMS_EPS)
    ot = ot * g_ref[...] * (1.0 - lambda_init)
    o_ref[...] = (ot.T * gate_ref[...].astype(F32)).astype(o_ref.dtype)


def _attention(q, k, v, gate, lam_vecs, subln_g, *, lambda_init, tq=1024, tk=512):
    S = q.shape[0]
    tq = min(tq, S)
    tk = min(tk, tq // 2)
    n_q = S // tq
    H = q.shape[1] // HEAD_W
    return pl.pallas_call(
        functools.partial(_attn_body, tq=tq, tk=tk, rc=min(128, tk // 2),
                          lambda_init=lambda_init),
        grid=(H, n_q),
        in_specs=[pl.BlockSpec((4, HEAD_DIM), lambda hh, i: (0, 0)),
                  pl.BlockSpec((HEAD_W, 1), lambda hh, i: (0, 0)),
                  pl.BlockSpec((tq, HEAD_W), lambda hh, i: (i, hh)),
                  pl.BlockSpec((tq, HEAD_W),
                               lambda hh, i: (jnp.minimum(i + 1, n_q - 1), hh)),
                  pl.BlockSpec((S, HEAD_W), lambda hh, i: (0, hh)),
                  pl.BlockSpec((S, HEAD_W), lambda hh, i: (0, hh)),
                  pl.BlockSpec((tq, HEAD_W), lambda hh, i: (i, hh))],
        out_specs=pl.BlockSpec((tq, HEAD_W), lambda hh, i: (i, hh)),
        out_shape=jax.ShapeDtypeStruct((S, H * HEAD_W), BF16),
        scratch_shapes=[pltpu.VMEM((2, HEAD_W, tq), BF16),
                        pltpu.VMEM((S // tk, V_ROWS, tk), BF16),
                        pltpu.VMEM((3, 2, tk, tq), F32),
                        pltpu.VMEM((3, 2, 1, tq), F32),
                        pltpu.VMEM((2, 2, tk, tq), BF16),
                        pltpu.VMEM((2, 2, 1, tq), F32),
                        pltpu.VMEM((2, 1, tq), F32),
                        pltpu.VMEM((2, V_ROWS, tq), F32)],
        compiler_params=_cparams("arbitrary", "arbitrary"),
        name="diff_attention",
    )(lam_vecs, subln_g.reshape(HEAD_W, 1), q, q, k, v, gate)


def _conv_body(val_ref, glu_ref, pval_ref, pglu_ref, gate_ref, dww_ref, dwb_ref, lng_ref,
               lnb_ref, pww_ref, pwb_ref, o_ref, ext_ref, sh_ref, y_ref, *, rc, cc):
    i = pl.program_id(0)
    T, C = y_ref.shape
    L = T + CONV_HALO
    prev = pval_ref[...].astype(F32) * pglu_ref[...].astype(F32)
    ext_ref[0:CONV_HALO, :] = jnp.where(i > 0, prev, 0.0)
    ext_ref[CONV_HALO:L, :] = val_ref[...].astype(F32) * glu_ref[...].astype(F32)
    ext_ref[L:L + SUBLANES, :] = jnp.zeros((SUBLANES, C), F32)

    for s in range(1, SUBLANES):
        for c0 in range(0, C, cc):
            sh_ref[s - 1, :, c0:c0 + cc] = ext_ref[s:s + L, c0:c0 + cc]

    def window(off, r0, c0):
        s, whole = off % SUBLANES, off - off % SUBLANES
        rows = slice(r0 + whole, r0 + whole + rc)
        return ext_ref[rows, c0:c0 + cc] if s == 0 else sh_ref[s - 1, rows, c0:c0 + cc]

    base = CONV_HALO - (CONV_K - 1)
    for r0 in range(0, T, rc):
        for c0 in range(0, C, cc):
            acc = jnp.zeros((rc, cc), F32) + dwb_ref[:, c0:c0 + cc]
            for j in range(CONV_K):
                acc = acc + dww_ref[j:j + 1, c0:c0 + cc] * window(base + j, r0, c0)
            y_ref[r0:r0 + rc, c0:c0 + cc] = acc

    a = _silu(_layer_norm_rows(y_ref[...], lng_ref[...], lnb_ref[...]))
    z = jnp.dot(a.astype(BF16), pww_ref[...], preferred_element_type=F32) + pwb_ref[...]
    o_ref[...] = (z * gate_ref[...].astype(F32)).astype(o_ref.dtype)


def _conv_branch(val, val_blk, glu, gate, gate_blk, dw_w, dw_b, ln_g, ln_b, pw_wb, layer, pw_b,
                 *, T=256):
    S, C = glu.shape
    T = min(T, S)
    hb = T // CONV_HALO
    row = lambda a: a.reshape(1, C)
    prev_rows = lambda i: jnp.maximum(i * hb - 1, 0)
    return pl.pallas_call(
        functools.partial(_conv_body, rc=64, cc=256),
        grid=(S // T,),
        in_specs=[pl.BlockSpec((T, C), lambda i: (i, val_blk)),
                  pl.BlockSpec((T, C), lambda i: (i, 0)),
                  pl.BlockSpec((CONV_HALO, C), lambda i: (prev_rows(i), val_blk)),
                  pl.BlockSpec((CONV_HALO, C), lambda i: (prev_rows(i), 0)),
                  pl.BlockSpec((T, C), lambda i: (i, gate_blk)),
                  pl.BlockSpec((CONV_K, C), lambda i: (0, 0)),
                  pl.BlockSpec((1, C), lambda i: (0, 0)),
                  pl.BlockSpec((1, C), lambda i: (0, 0)),
                  pl.BlockSpec((1, C), lambda i: (0, 0)),
                  pl.BlockSpec((None, C, C), lambda i: (layer, 0, 0)),
                  pl.BlockSpec((1, C), lambda i: (0, 0))],
        out_specs=pl.BlockSpec((T, C), lambda i: (i, 0)),
        out_shape=jax.ShapeDtypeStruct((S, C), BF16),
        scratch_shapes=[pltpu.VMEM((T + CONV_HALO + SUBLANES, C), F32),
                        pltpu.VMEM((SUBLANES - 1, T + CONV_HALO, C), F32),
                        pltpu.VMEM((T, C), F32)],
        compiler_params=_cparams("parallel"),
        name="conv_branch",
    )(val, glu, val, glu, gate, dw_w, row(dw_b), row(ln_g), row(ln_b), pw_wb, row(pw_b))


def _sgu_body(u_ref, v_ref, gate_ref, lng_ref, lnb_ref, w_ref, b_ref, o_ref, vn_ref):
    T, W = vn_ref.shape
    vn_ref[...] = _layer_norm_rows(v_ref[...].astype(F32), lng_ref[...], lnb_ref[...]).astype(BF16)
    r_id = lax.broadcasted_iota(jnp.int32, (CHUNK, CHUNK), 0)
    c_id = lax.broadcasted_iota(jnp.int32, (CHUNK, CHUNK), 1)
    causal = c_id <= r_id
    for g in range(W // SGU_GROUP_W):
        cols = slice(g * SGU_GROUP_W, (g + 1) * SGU_GROUP_W)
        wg = jnp.where(causal, w_ref[g], 0.0).astype(BF16)
        bias = b_ref[g]
        for n in range(T // CHUNK):
            rows = slice(n * CHUNK, (n + 1) * CHUNK)
            mixed = jnp.dot(wg, vn_ref[rows, cols], preferred_element_type=F32) + bias
            o_ref[rows, cols] = (u_ref[rows, cols].astype(F32) * mixed
                                 * gate_ref[rows, cols].astype(F32)).astype(o_ref.dtype)


def _sgu_branch(uv, gate, gate_blk, ln_g, ln_b, w_s, b_s, *, T=512):
    S = uv.shape[0]
    G = w_s.shape[0]
    W = G * SGU_GROUP_W
    T = min(T, S)
    return pl.pallas_call(
        _sgu_body,
        grid=(S // T,),
        in_specs=[pl.BlockSpec((T, W), lambda i: (i, 0)),
                  pl.BlockSpec((T, W), lambda i: (i, 1)),
                  pl.BlockSpec((T, W), lambda i: (i, gate_blk)),
                  pl.BlockSpec((1, W), lambda i: (0, 0)),
                  pl.BlockSpec((1, W), lambda i: (0, 0)),
                  pl.BlockSpec((G, CHUNK, CHUNK), lambda i: (0, 0, 0)),
                  pl.BlockSpec((G, CHUNK, 1), lambda i: (0, 0, 0))],
        out_specs=pl.BlockSpec((T, W), lambda i: (i, 0)),
        out_shape=jax.ShapeDtypeStruct((S, W), BF16),
        scratch_shapes=[pltpu.VMEM((T, W), BF16)],
        compiler_params=_cparams("parallel"),
        name="sgu_branch",
    )(uv, uv, gate, ln_g.reshape(1, W), ln_b.reshape(1, W), w_s, b_s.reshape(G, CHUNK, 1))


def _outproj_body(a_ref, c_ref, s_ref, wa_ref, wc_ref, ws_ref, x_ref, o_ref, *, alpha):
    y = jnp.dot(a_ref[...], wa_ref[...], preferred_element_type=F32)
    y = y + jnp.dot(c_ref[...], wc_ref[...], preferred_element_type=F32)
    y = y + jnp.dot(s_ref[...], ws_ref[...], preferred_element_type=F32)
    o_ref[...] = alpha * x_ref[...] + y


def _out_projection(attn, conv, sgu, w, layer, x, *, alpha, tm=1024, tn=1024):
    S, D = x.shape
    tm = min(tm, S)
    wa, wc, ws = attn.shape[1], conv.shape[1], sgu.shape[1]
    return pl.pallas_call(
        functools.partial(_outproj_body, alpha=alpha),
        grid=(S // tm, D // tn),
        in_specs=[pl.BlockSpec((tm, wa), lambda i, j: (i, 0)),
                  pl.BlockSpec((tm, wc), lambda i, j: (i, 0)),
                  pl.BlockSpec((tm, ws), lambda i, j: (i, 0)),
                  pl.BlockSpec((None, wa, tn), lambda i, j: (layer, 0, j)),
                  pl.BlockSpec((None, wc, tn), lambda i, j: (layer, wa // wc, j)),
                  pl.BlockSpec((None, ws, tn), lambda i, j: (layer, (wa + wc) // ws, j)),
                  pl.BlockSpec((tm, tn), lambda i, j: (i, j))],
        out_specs=pl.BlockSpec((tm, tn), lambda i, j: (i, j)),
        out_shape=jax.ShapeDtypeStruct((S, D), F32),
        compiler_params=_cparams("parallel", "arbitrary"),
        name="out_projection",
    )(attn, conv, sgu, w, w, w, x)


def _postln_body(r_ref, g_ref, b_ref, *o_refs):
    y = _layer_norm_rows(r_ref[...], g_ref[...], b_ref[...])
    for o_ref in o_refs:
        o_ref[...] = y.astype(o_ref.dtype)


def _post_layer_norm(r, g, b, *, with_bf16, tr=512):
    S, D = r.shape
    tr = min(tr, S)
    dtypes = (F32, BF16) if with_bf16 else (F32,)
    return pl.pallas_call(
        _postln_body,
        grid=(S // tr,),
        in_specs=[pl.BlockSpec((tr, D), lambda i: (i, 0)),
                  pl.BlockSpec((1, D), lambda i: (0, 0)),
                  pl.BlockSpec((1, D), lambda i: (0, 0))],
        out_specs=[pl.BlockSpec((tr, D), lambda i: (i, 0)) for _ in dtypes],
        out_shape=[jax.ShapeDtypeStruct((S, D), dt) for dt in dtypes],
        compiler_params=_cparams("parallel"),
        name="post_layer_norm",
    )(r, g.reshape(1, D), b.reshape(1, D))


def kernel(x, positions, w_in, attn_lambda, attn_subln_g, conv_dw_w, conv_dw_b, conv_ln_g, conv_ln_b, conv_pw_w, conv_pw_b, sgu_ln_g, sgu_ln_b, sgu_w, sgu_b, w_out, post_ln_g, post_ln_b):
    B, S, D = x.shape
    depth = w_in.shape[0]
    assert B == 1 and S % 1024 == 0 and D % 2048 == 0
    u = D // 4
    alpha = (2 * depth) ** 0.25
    groups = {"rope_q": (0, 1), "rope_k": (2, 3),
              "ident": (4, 5, 8),
              "silu": (6, 7, 10, 13),
              "sigmoid": (9,), "gelu": (11, 12)}

    cos, sin = _rope_tables(positions)
    xf = x.reshape(S, D)
    xb = _to_bf16(x)[0]
    w_out_b, pw_w_b = _to_bf16(w_out), _to_bf16(conv_pw_w)
    for layer in range(depth):
        lambda_init = 0.8 - 0.6 * math.exp(-0.3 * layer)
        h = {kind: _in_projection(xb, w_in, layer, cos, sin, kind=kind, units=units, unit_w=u)
             for kind, units in groups.items()}
        attn = _attention(h["rope_q"], h["rope_k"], h["ident"], h["silu"], attn_lambda[layer],
                          attn_subln_g[layer], lambda_init=lambda_init)
        conv = _conv_branch(h["ident"], 2, h["sigmoid"], h["silu"], 2, conv_dw_w[layer],
                            conv_dw_b[layer], conv_ln_g[layer], conv_ln_b[layer],
                            pw_w_b, layer, conv_pw_b[layer])
        sgu = _sgu_branch(h["gelu"], h["silu"], 3, sgu_ln_g[layer], sgu_ln_b[layer],
                          sgu_w[layer], sgu_b[layer])
        r = _out_projection(attn, conv, sgu, w_out_b, layer, xf, alpha=alpha)
        outs = _post_layer_norm(r, post_ln_g[layer], post_ln_b[layer],
                                with_bf16=layer + 1 < depth)
        xf, xb = outs[0], outs[-1]
    return xf.reshape(B, S, D)
```

```python
import functools
import math

import jax
import jax.numpy as jnp
from jax import lax
from jax.experimental import pallas as pl
from jax.experimental.pallas import tpu as pltpu

F32 = jnp.float32
BF16 = jnp.bfloat16

HEAD_DIM = 64
HEAD_W = 2 * HEAD_DIM
CONV_K = 31
CONV_HALO = 32
CHUNK = 128
SGU_GROUP_W = 128
ROPE_THETA = 10000.0
LN_EPS = 1e-5
RMS_EPS = 1e-5
MASK_VALUE = -1e30
Q_SCALE = HEAD_DIM ** -0.5 * math.log2(math.e)
SUBLANES = 8
BF16_SUBLANES = 16
V7X_VMEM_BYTES = 64 * 1024 * 1024
VMEM_LIMIT = V7X_VMEM_BYTES - 8 * 1024 * 1024


def _cparams(*sem):
    return pltpu.CompilerParams(dimension_semantics=sem, vmem_limit_bytes=VMEM_LIMIT)


def _layer_norm_rows(x, g, b):
    mu = jnp.mean(x, axis=-1, keepdims=True)
    xc = x - mu
    var = jnp.mean(xc * xc, axis=-1, keepdims=True)
    return xc * lax.rsqrt(var + LN_EPS) * g + b


def _silu(x):
    return x * jax.nn.sigmoid(x)


def _gelu(x):
    return 0.5 * x * (1.0 + lax.erf(x * (1.0 / math.sqrt(2.0))))


def _rope_body(pos_ref, inv_ref, cos_ref, sin_ref):
    ang = pos_ref[...].astype(F32) * inv_ref[...]
    lane = lax.broadcasted_iota(jnp.int32, ang.shape, 1)
    first_half = (lane % HEAD_DIM) < (HEAD_DIM // 2)
    s = jnp.sin(ang)
    cos_ref[...] = jnp.cos(ang)
    sin_ref[...] = jnp.where(first_half, -s, s)


def _rope_tables(positions):
    S = positions.shape[-1]
    ts = min(S, 1024)
    inv = ROPE_THETA ** (-jnp.arange(0, HEAD_DIM, 2, dtype=F32) / HEAD_DIM)
    inv = jnp.tile(inv, HEAD_W // (HEAD_DIM // 2))[None, :]
    return pl.pallas_call(
        _rope_body,
        grid=(S // ts,),
        in_specs=[pl.BlockSpec((ts, 1), lambda i: (i, 0)),
                  pl.BlockSpec((1, HEAD_W), lambda i: (0, 0))],
        out_specs=[pl.BlockSpec((ts, HEAD_W), lambda i: (i, 0))] * 2,
        out_shape=[jax.ShapeDtypeStruct((S, HEAD_W), F32)] * 2,
        compiler_params=_cparams("parallel"),
        name="rope_tables",
    )(positions.reshape(S, 1), inv)


def _rope(t, cos, sin, scale):
    lane = lax.broadcasted_iota(jnp.int32, t.shape, 1)
    first_half = (lane % HEAD_DIM) < (HEAD_DIM // 2)
    rot = jnp.where(first_half,
                    pltpu.roll(t, HEAD_W - HEAD_DIM // 2, 1),
                    pltpu.roll(t, HEAD_DIM // 2, 1))
    return (t * cos + rot * sin) * scale


_POINTWISE = {"ident": lambda t: t, "silu": _silu, "sigmoid": jax.nn.sigmoid, "gelu": _gelu}
_ROPE_SCALE = {"rope_q": Q_SCALE, "rope_k": 1.0}


def _inproj_body(x_ref, w_ref, cos_ref, sin_ref, o_ref, wb_ref, *, kind, rm, kchunk):
    tm, tn = o_ref.shape

    @pl.when(pl.program_id(1) == 0)
    def _():
        for k0 in range(0, w_ref.shape[0], kchunk):
            wb_ref[k0:k0 + kchunk, :] = w_ref[k0:k0 + kchunk, :].astype(wb_ref.dtype)

    for r0 in range(0, tm, rm):
        rows = slice(r0, r0 + rm)
        a = jnp.dot(x_ref[rows, :], wb_ref[...], preferred_element_type=F32)
        if kind in _ROPE_SCALE:
            cos, sin = cos_ref[rows, :], sin_ref[rows, :]
            for c0 in range(0, tn, HEAD_W):
                o_ref[rows, c0:c0 + HEAD_W] = _rope(a[:, c0:c0 + HEAD_W], cos, sin,
                                                    _ROPE_SCALE[kind]).astype(o_ref.dtype)
        else:
            o_ref[rows, :] = _POINTWISE[kind](a).astype(o_ref.dtype)


def _cast_body(x_ref, o_ref):
    o_ref[...] = x_ref[...].astype(o_ref.dtype)


def _to_bf16(a, *, block_bytes=8 * 1024 * 1024):
    L, R, C = a.shape
    tr = R
    while tr * C * 4 > block_bytes and tr % 16 == 0:
        tr //= 2
    return pl.pallas_call(
        _cast_body,
        grid=(L, R // tr),
        in_specs=[pl.BlockSpec((None, tr, C), lambda l, i: (l, i, 0))],
        out_specs=pl.BlockSpec((None, tr, C), lambda l, i: (l, i, 0)),
        out_shape=jax.ShapeDtypeStruct(a.shape, BF16),
        compiler_params=_cparams("parallel", "parallel"),
        name="to_bf16",
    )(a)


def _in_projection(xb, w, layer, cos, sin, *, kind, units, unit_w, tm=1024, tn=512, rm=128):
    S, D = xb.shape
    tm = min(tm, S)
    per_unit = unit_w // tn
    ncols = len(units) * unit_w

    def w_tile(j):
        first = jnp.int32(units[0] * per_unit)
        for n, unit in enumerate(units[1:], 1):
            first = jnp.where(j // per_unit == n, unit * per_unit, first)
        return first + j % per_unit

    return pl.pallas_call(
        functools.partial(_inproj_body, kind=kind, rm=rm, kchunk=512),
        grid=(ncols // tn, S // tm),
        in_specs=[pl.BlockSpec((tm, D), lambda j, i: (i, 0)),
                  pl.BlockSpec((None, D, tn), lambda j, i: (layer, 0, w_tile(j))),
                  pl.BlockSpec((tm, HEAD_W), lambda j, i: (i, 0)),
                  pl.BlockSpec((tm, HEAD_W), lambda j, i: (i, 0))],
        out_specs=pl.BlockSpec((tm, tn), lambda j, i: (i, j)),
        out_shape=jax.ShapeDtypeStruct((S, ncols), BF16),
        scratch_shapes=[pltpu.VMEM((D, tn), BF16)],
        compiler_params=_cparams("arbitrary", "arbitrary"),
        name="in_projection_" + kind,
    )(xb, w, cos, sin)


FIRST = 2
V_ROWS = HEAD_W + BF16_SUBLANES


def _attn_body(lam_ref, g_ref, q_ref, qn_ref, k_ref, v_ref, gate_ref, o_ref,
               qt_ref, vt_ref, st_ref, bm_ref, pt_ref, al_ref, m_ref, acc_ref,
               *, tq, tk, rc, lambda_init):
    qi = pl.program_id(1)
    halves = range(2)
    n_ch = tk // rc
    kc = 2 * rc
    nd = tq // tk
    assert tk % kc == 0 and tq % tk == 0 and nd % 2 == 0

    def by_vreg(a):
        return a.reshape(a.shape[0] // SUBLANES, SUBLANES, a.shape[1])

    all_q = slice(0, tq)

    def score(b, slot, qs=all_q):
        k = k_ref[pl.ds(pl.multiple_of(b * tk, tk), tk), :]
        bm = []
        for c in halves:
            st = jnp.dot(k, qt_ref[c, :, qs], preferred_element_type=F32)
            st_ref[slot, c, :, qs] = st
            bm.append(jnp.max(by_vreg(st), axis=0))
        return bm

    def score_end(slot, bm, qs=all_q):
        for c in halves:
            bm_ref[slot, c, :, qs] = jnp.max(bm[c], axis=0, keepdims=True)

    def load_scores(slot, c, r, masked, qs):
        st = st_ref[slot, c, r * rc:(r + 1) * rc, qs]
        if masked is not None:
            k_id = lax.broadcasted_iota(jnp.int32, st.shape, 0) + (r * rc + masked)
            q_id = lax.broadcasted_iota(jnp.int32, st.shape, 1) + qs.start
            st = jnp.where(k_id <= q_id, st, MASK_VALUE)
        return st

    def prob_begin(slot, masked, qs):
        if masked is not None:
            bmax = []
            for c in halves:
                bm = jnp.max(by_vreg(load_scores(slot, c, 0, masked, qs)), axis=0)
                for r in range(1, n_ch):
                    bm = jnp.maximum(bm, jnp.max(by_vreg(load_scores(slot, c, r, masked, qs)),
                                                 axis=0))
                bmax.append(jnp.max(bm, axis=0, keepdims=True))
        else:
            bmax = [bm_ref[slot, c, :, qs] for c in halves]
        m_prev = [m_ref[c, :, qs] for c in halves]
        m_cur = [jnp.maximum(m_prev[c], bmax[c]) for c in halves]
        alpha = [jnp.exp2(m_prev[c] - m_cur[c]) for c in halves]
        return m_cur, alpha

    def prob_chunk(slot, par, r, masked, qs, m_cur):
        for c in halves:
            x = load_scores(slot, c, r, masked, qs) - m_cur[c]
            pt_ref[par, c, r * rc:(r + 1) * rc, qs] = jnp.exp2(x.astype(pt_ref.dtype))

    def prob_end(par, qs, m_cur, alpha):
        for c in halves:
            al_ref[par, c, :, qs] = alpha[c]
            m_ref[c, :, qs] = m_cur[c]

    def value_chunk(b, par, j, qs, part):
        keys = slice(j * kc, (j + 1) * kc)
        vt = vt_ref[b, :, keys]
        for c in halves:
            pv = jnp.dot(vt, pt_ref[par, c, keys, qs], preferred_element_type=F32)
            part[c] = pv if part[c] is None else part[c] + pv

    def value_end(par, qs, part):
        for c in halves:
            acc_ref[c, :, qs] = al_ref[par, c, :, qs] * acc_ref[c, :, qs] + part[c]

    def consume(b, slot, par, masked, qs=all_q, next_slot=None, next_qs=all_q):
        m_cur, alpha = prob_begin(slot, masked, qs)
        part = [None, None]
        bm = score(b + 1, next_slot, next_qs) if next_slot is not None else None
        for r in range(n_ch):
            prob_chunk(slot, par, r, masked, qs, m_cur)
            if (r + 1) * rc % kc == 0:
                value_chunk(b, par, (r + 1) * rc // kc - 1, qs, part)
        if bm is not None:
            score_end(next_slot, bm, next_qs)
        prob_end(par, qs, m_cur, alpha)
        value_end(par, qs, part)

    def start_block(ref):
        qt = ref[...].astype(F32).T
        row = lax.broadcasted_iota(jnp.int32, qt.shape, 0)
        qt_ref[0] = jnp.where(row < HEAD_DIM, qt, 0.0).astype(qt_ref.dtype)
        qt_ref[1] = jnp.where(row >= HEAD_DIM, qt, 0.0).astype(qt_ref.dtype)
        score_end(FIRST, score(0, FIRST))

    @pl.when(qi == 0)
    def _():
        ones = jnp.ones((V_ROWS - HEAD_W, tk), vt_ref.dtype)
        for kb in range(vt_ref.shape[0]):
            vt_ref[kb, 0:HEAD_W, :] = (v_ref[kb * tk:(kb + 1) * tk, :].astype(F32).T
                                       .astype(vt_ref.dtype))
            vt_ref[kb, HEAD_W:V_ROWS, :] = ones
        start_block(q_ref)

    m_ref[...] = jnp.full(m_ref.shape, MASK_VALUE, F32)
    acc_ref[...] = jnp.zeros(acc_ref.shape, F32)

    @pl.when(qi > 0)
    def _():
        consume(0, FIRST, 0, None, next_slot=1)

    def body(b, carry):
        lax.cond(b % 2 == 0,
                 lambda: consume(b, 0, 0, None, next_slot=1),
                 lambda: consume(b, 1, 1, None, next_slot=0))
        return carry

    lax.fori_loop(1, nd * qi, body, 0)

    def seen_by(d):
        return slice(d * tk, tq)

    def diagonal(d, slot, with_next):
        b, par = nd * qi + d, d % 2
        nxt = dict(next_slot=(d + 1) % 2, next_qs=seen_by(d + 1)) if with_next else {}
        consume(b, slot, par, d * tk, qs=slice(d * tk, (d + 1) * tk), **nxt)
        if d + 1 < nd:
            consume(b, slot, par, None, qs=seen_by(d + 1))

    for cond, slot in ((qi == 0, FIRST), (qi > 0, 0)):
        @pl.when(cond)
        def _():
            diagonal(0, slot, True)

    for d in range(1, nd - 1):
        diagonal(d, d % 2, True)

    start_block(qn_ref)
    diagonal(nd - 1, (nd - 1) % 2, False)

    lv = lam_ref[...]
    lam = (jnp.exp(jnp.sum(lv[0:1] * lv[1:2], axis=1, keepdims=True))
           - jnp.exp(jnp.sum(lv[2:3] * lv[3:4], axis=1, keepdims=True)) + lambda_init)
    num = [acc_ref[c, 0:HEAD_W, :] for c in halves]
    den = [acc_ref[c, HEAD_W:HEAD_W + 1, :] for c in halves]
    ot = num[0] * (1.0 / den[0]) - lam * (num[1] * (1.0 / den[1]))
    ot = ot * lax.rsqrt(jnp.mean(ot * ot, axis=0, keepdims=True) + RMS_EPS)
    ot = ot * g_ref[...] * (1.0 - lambda_init)
    o_ref[...] = (ot.T * gate_ref[...].astype(F32)).astype(o_ref.dtype)


def _attention(q, k, v, gate, lam_vecs, subln_g, *, lambda_init, tq=1024, tk=512):
    S = q.shape[0]
    tq = min(tq, S)
    tk = min(tk, tq // 2)
    n_q = S // tq
    H = q.shape[1] // HEAD_W
    return pl.pallas_call(
        functools.partial(_attn_body, tq=tq, tk=tk, rc=min(128, tk // 2),
                          lambda_init=lambda_init),
        grid=(H, n_q),
        in_specs=[pl.BlockSpec((4, HEAD_DIM), lambda hh, i: (0, 0)),
                  pl.BlockSpec((HEAD_W, 1), lambda hh, i: (0, 0)),
                  pl.BlockSpec((tq, HEAD_W), lambda hh, i: (i, hh)),
                  pl.BlockSpec((tq, HEAD_W),
                               lambda hh, i: (jnp.minimum(i + 1, n_q - 1), hh)),
                  pl.BlockSpec((S, HEAD_W), lambda hh, i: (0, hh)),
                  pl.BlockSpec((S, HEAD_W), lambda hh, i: (0, hh)),
                  pl.BlockSpec((tq, HEAD_W), lambda hh, i: (i, hh))],
        out_specs=pl.BlockSpec((tq, HEAD_W), lambda hh, i: (i, hh)),
        out_shape=jax.ShapeDtypeStruct((S, H * HEAD_W), BF16),
        scratch_shapes=[pltpu.VMEM((2, HEAD_W, tq), BF16),
                        pltpu.VMEM((S // tk, V_ROWS, tk), BF16),
                        pltpu.VMEM((3, 2, tk, tq), F32),
                        pltpu.VMEM((3, 2, 1, tq), F32),
                        pltpu.VMEM((2, 2, tk, tq), BF16),
                        pltpu.VMEM((2, 2, 1, tq), F32),
                        pltpu.VMEM((2, 1, tq), F32),
                        pltpu.VMEM((2, V_ROWS, tq), F32)],
        compiler_params=_cparams("arbitrary", "arbitrary"),
        name="diff_attention",
    )(lam_vecs, subln_g.reshape(HEAD_W, 1), q, q, k, v, gate)


def _conv_body(val_ref, glu_ref, pval_ref, pglu_ref, gate_ref, dww_ref, dwb_ref, lng_ref,
               lnb_ref, pww_ref, pwb_ref, o_ref, ext_ref, sh_ref, y_ref, *, rc, cc):
    i = pl.program_id(0)
    T, C = y_ref.shape
    L = T + CONV_HALO
    prev = pval_ref[...].astype(F32) * pglu_ref[...].astype(F32)
    ext_ref[0:CONV_HALO, :] = jnp.where(i > 0, prev, 0.0)
    ext_ref[CONV_HALO:L, :] = val_ref[...].astype(F32) * glu_ref[...].astype(F32)
    ext_ref[L:L + SUBLANES, :] = jnp.zeros((SUBLANES, C), F32)

    for s in range(1, SUBLANES):
        for c0 in range(0, C, cc):
            sh_ref[s - 1, :, c0:c0 + cc] = ext_ref[s:s + L, c0:c0 + cc]

    def window(off, r0, c0):
        s, whole = off % SUBLANES, off - off % SUBLANES
        rows = slice(r0 + whole, r0 + whole + rc)
        return ext_ref[rows, c0:c0 + cc] if s == 0 else sh_ref[s - 1, rows, c0:c0 + cc]

    base = CONV_HALO - (CONV_K - 1)
    for r0 in range(0, T, rc):
        for c0 in range(0, C, cc):
            acc = jnp.zeros((rc, cc), F32) + dwb_ref[:, c0:c0 + cc]
            for j in range(CONV_K):
                acc = acc + dww_ref[j:j + 1, c0:c0 + cc] * window(base + j, r0, c0)
            y_ref[r0:r0 + rc, c0:c0 + cc] = acc

    a = _silu(_layer_norm_rows(y_ref[...], lng_ref[...], lnb_ref[...]))
    z = jnp.dot(a.astype(BF16), pww_ref[...], preferred_element_type=F32) + pwb_ref[...]
    o_ref[...] = (z * gate_ref[...].astype(F32)).astype(o_ref.dtype)


def _conv_branch(val, val_blk, glu, gate, gate_blk, dw_w, dw_b, ln_g, ln_b, pw_wb, layer, pw_b,
                 *, T=512):
    S, C = glu.shape
    T = min(T, S)
    hb = T // CONV_HALO
    row = lambda a: a.reshape(1, C)
    prev_rows = lambda i: jnp.maximum(i * hb - 1, 0)
    return pl.pallas_call(
        functools.partial(_conv_body, rc=64, cc=256),
        grid=(S // T,),
        in_specs=[pl.BlockSpec((T, C), lambda i: (i, val_blk)),
                  pl.BlockSpec((T, C), lambda i: (i, 0)),
                  pl.BlockSpec((CONV_HALO, C), lambda i: (prev_rows(i), val_blk)),
                  pl.BlockSpec((CONV_HALO, C), lambda i: (prev_rows(i), 0)),
                  pl.BlockSpec((T, C), lambda i: (i, gate_blk)),
                  pl.BlockSpec((CONV_K, C), lambda i: (0, 0)),
                  pl.BlockSpec((1, C), lambda i: (0, 0)),
                  pl.BlockSpec((1, C), lambda i: (0, 0)),
                  pl.BlockSpec((1, C), lambda i: (0, 0)),
                  pl.BlockSpec((None, C, C), lambda i: (layer, 0, 0)),
                  pl.BlockSpec((1, C), lambda i: (0, 0))],
        out_specs=pl.BlockSpec((T, C), lambda i: (i, 0)),
        out_shape=jax.ShapeDtypeStruct((S, C), BF16),
        scratch_shapes=[pltpu.VMEM((T + CONV_HALO + SUBLANES, C), F32),
                        pltpu.VMEM((SUBLANES - 1, T + CONV_HALO, C), F32),
                        pltpu.VMEM((T, C), F32)],
        compiler_params=_cparams("parallel"),
        name="conv_branch",
    )(val, glu, val, glu, gate, dw_w, row(dw_b), row(ln_g), row(ln_b), pw_wb, row(pw_b))


def _sgu_body(u_ref, v_ref, gate_ref, lng_ref, lnb_ref, w_ref, b_ref, o_ref, vn_ref):
    T, W = vn_ref.shape
    vn_ref[...] = _layer_norm_rows(v_ref[...].astype(F32), lng_ref[...], lnb_ref[...]).astype(BF16)
    r_id = lax.broadcasted_iota(jnp.int32, (CHUNK, CHUNK), 0)
    c_id = lax.broadcasted_iota(jnp.int32, (CHUNK, CHUNK), 1)
    causal = c_id <= r_id
    for g in range(W // SGU_GROUP_W):
        cols = slice(g * SGU_GROUP_W, (g + 1) * SGU_GROUP_W)
        wg = jnp.where(causal, w_ref[g], 0.0).astype(BF16)
        bias = b_ref[g]
        for n in range(T // CHUNK):
            rows = slice(n * CHUNK, (n + 1) * CHUNK)
            mixed = jnp.dot(wg, vn_ref[rows, cols], preferred_element_type=F32) + bias
            o_ref[rows, cols] = (u_ref[rows, cols].astype(F32) * mixed
                                 * gate_ref[rows, cols].astype(F32)).astype(o_ref.dtype)


def _sgu_branch(uv, gate, gate_blk, ln_g, ln_b, w_s, b_s, *, T=512):
    S = uv.shape[0]
    G = w_s.shape[0]
    W = G * SGU_GROUP_W
    T = min(T, S)
    return pl.pallas_call(
        _sgu_body,
        grid=(S // T,),
        in_specs=[pl.BlockSpec((T, W), lambda i: (i, 0)),
                  pl.BlockSpec((T, W), lambda i: (i, 1)),
                  pl.BlockSpec((T, W), lambda i: (i, gate_blk)),
                  pl.BlockSpec((1, W), lambda i: (0, 0)),
                  pl.BlockSpec((1, W), lambda i: (0, 0)),
                  pl.BlockSpec((G, CHUNK, CHUNK), lambda i: (0, 0, 0)),
                  pl.BlockSpec((G, CHUNK, 1), lambda i: (0, 0, 0))],
        out_specs=pl.BlockSpec((T, W), lambda i: (i, 0)),
        out_shape=jax.ShapeDtypeStruct((S, W), BF16),
        scratch_shapes=[pltpu.VMEM((T, W), BF16)],
        compiler_params=_cparams("parallel"),
        name="sgu_branch",
    )(uv, uv, gate, ln_g.reshape(1, W), ln_b.reshape(1, W), w_s, b_s.reshape(G, CHUNK, 1))


def _outproj_body(a_ref, c_ref, s_ref, wa_ref, wc_ref, ws_ref, x_ref, o_ref, *, alpha):
    y = jnp.dot(a_ref[...], wa_ref[...], preferred_element_type=F32)
    y = y + jnp.dot(c_ref[...], wc_ref[...], preferred_element_type=F32)
    y = y + jnp.dot(s_ref[...], ws_ref[...], preferred_element_type=F32)
    o_ref[...] = alpha * x_ref[...] + y


def _out_projection(attn, conv, sgu, w, layer, x, *, alpha, tm=1024, tn=1024):
    S, D = x.shape
    tm = min(tm, S)
    wa, wc, ws = attn.shape[1], conv.shape[1], sgu.shape[1]
    return pl.pallas_call(
        functools.partial(_outproj_body, alpha=alpha),
        grid=(S // tm, D // tn),
        in_specs=[pl.BlockSpec((tm, wa), lambda i, j: (i, 0)),
                  pl.BlockSpec((tm, wc), lambda i, j: (i, 0)),
                  pl.BlockSpec((tm, ws), lambda i, j: (i, 0)),
                  pl.BlockSpec((None, wa, tn), lambda i, j: (layer, 0, j)),
                  pl.BlockSpec((None, wc, tn), lambda i, j: (layer, wa // wc, j)),
                  pl.BlockSpec((None, ws, tn), lambda i, j: (layer, (wa + wc) // ws, j)),
                  pl.BlockSpec((tm, tn), lambda i, j: (i, j))],
        out_specs=pl.BlockSpec((tm, tn), lambda i, j: (i, j)),
        out_shape=jax.ShapeDtypeStruct((S, D), F32),
        compiler_params=_cparams("parallel", "arbitrary"),
        name="out_projection",
    )(attn, conv, sgu, w, w, w, x)


def _postln_body(r_ref, g_ref, b_ref, *o_refs):
    y = _layer_norm_rows(r_ref[...], g_ref[...], b_ref[...])
    for o_ref in o_refs:
        o_ref[...] = y.astype(o_ref.dtype)


def _post_layer_norm(r, g, b, *, with_bf16, tr=512):
    S, D = r.shape
    tr = min(tr, S)
    dtypes = (F32, BF16) if with_bf16 else (F32,)
    return pl.pallas_call(
        _postln_body,
        grid=(S // tr,),
        in_specs=[pl.BlockSpec((tr, D), lambda i: (i, 0)),
                  pl.BlockSpec((1, D), lambda i: (0, 0)),
                  pl.BlockSpec((1, D), lambda i: (0, 0))],
        out_specs=[pl.BlockSpec((tr, D), lambda i: (i, 0)) for _ in dtypes],
        out_shape=[jax.ShapeDtypeStruct((S, D), dt) for dt in dtypes],
        compiler_params=_cparams("parallel"),
        name="post_layer_norm",
    )(r, g.reshape(1, D), b.reshape(1, D))


def kernel(x, positions, w_in, attn_lambda, attn_subln_g, conv_dw_w, conv_dw_b, conv_ln_g, conv_ln_b, conv_pw_w, conv_pw_b, sgu_ln_g, sgu_ln_b, sgu_w, sgu_b, w_out, post_ln_g, post_ln_b):
    B, S, D = x.shape
    depth = w_in.shape[0]
    assert B == 1 and S % 1024 == 0 and D % 2048 == 0
    u = D // 4
    alpha = (2 * depth) ** 0.25
    groups = {"rope_q": (0, 1), "rope_k": (2, 3),
              "ident": (4, 5, 8),
              "silu": (6, 7, 10, 13),
              "sigmoid": (9,), "gelu": (11, 12)}

    cos, sin = _rope_tables(positions)
    xf = x.reshape(S, D)
    xb = _to_bf16(x)[0]
    w_out_b, pw_w_b = _to_bf16(w_out), _to_bf16(conv_pw_w)
    for layer in range(depth):
        lambda_init = 0.8 - 0.6 * math.exp(-0.3 * layer)
        h = {kind: _in_projection(xb, w_in, layer, cos, sin, kind=kind, units=units, unit_w=u)
             for kind, units in groups.items()}
        attn = _attention(h["rope_q"], h["rope_k"], h["ident"], h["silu"], attn_lambda[layer],
                          attn_subln_g[layer], lambda_init=lambda_init)
        conv = _conv_branch(h["ident"], 2, h["sigmoid"], h["silu"], 2, conv_dw_w[layer],
                            conv_dw_b[layer], conv_ln_g[layer], conv_ln_b[layer],
                            pw_w_b, layer, conv_pw_b[layer])
        sgu = _sgu_branch(h["gelu"], h["silu"], 3, sgu_ln_g[layer], sgu_ln_b[layer],
                          sgu_w[layer], sgu_b[layer])
        r = _out_projection(attn, conv, sgu, w_out_b, layer, xf, alpha=alpha)
        outs = _post_layer_norm(r, post_ln_g[layer], post_ln_b[layer],
                                with_bf16=layer + 1 < depth)
        xf, xb = outs[0], outs[-1]
    return xf.reshape(B, S, D)
```

```python
import functools
import math

import jax
import jax.numpy as jnp
from jax import lax
from jax.experimental import pallas as pl
from jax.experimental.pallas import tpu as pltpu

F32 = jnp.float32
BF16 = jnp.bfloat16

HEAD_DIM = 64
HEAD_W = 2 * HEAD_DIM
CONV_K = 31
CONV_HALO = 32
CHUNK = 128
SGU_GROUP_W = 128
ROPE_THETA = 10000.0
LN_EPS = 1e-5
RMS_EPS = 1e-5
MASK_VALUE = -1e30
Q_SCALE = HEAD_DIM ** -0.5 * math.log2(math.e)
SUBLANES = 8
BF16_SUBLANES = 16
V7X_VMEM_BYTES = 64 * 1024 * 1024
VMEM_LIMIT = V7X_VMEM_BYTES - 8 * 1024 * 1024


def _cparams(*sem):
    return pltpu.CompilerParams(dimension_semantics=sem, vmem_limit_bytes=VMEM_LIMIT)


def _layer_norm_rows(x, g, b):
    mu = jnp.mean(x, axis=-1, keepdims=True)
    xc = x - mu
    var = jnp.mean(xc * xc, axis=-1, keepdims=True)
    return xc * lax.rsqrt(var + LN_EPS) * g + b


def _silu(x):
    return x * jax.nn.sigmoid(x)


def _gelu(x):
    return 0.5 * x * (1.0 + lax.erf(x * (1.0 / math.sqrt(2.0))))


def _rope_body(pos_ref, inv_ref, cos_ref, sin_ref):
    ang = pos_ref[...].astype(F32) * inv_ref[...]
    lane = lax.broadcasted_iota(jnp.int32, ang.shape, 1)
    first_half = (lane % HEAD_DIM) < (HEAD_DIM // 2)
    s = jnp.sin(ang)
    cos_ref[...] = jnp.cos(ang)
    sin_ref[...] = jnp.where(first_half, -s, s)


def _rope_tables(positions):
    S = positions.shape[-1]
    ts = min(S, 1024)
    inv = ROPE_THETA ** (-jnp.arange(0, HEAD_DIM, 2, dtype=F32) / HEAD_DIM)
    inv = jnp.tile(inv, HEAD_W // (HEAD_DIM // 2))[None, :]
    return pl.pallas_call(
        _rope_body,
        grid=(S // ts,),
        in_specs=[pl.BlockSpec((ts, 1), lambda i: (i, 0)),
                  pl.BlockSpec((1, HEAD_W), lambda i: (0, 0))],
        out_specs=[pl.BlockSpec((ts, HEAD_W), lambda i: (i, 0))] * 2,
        out_shape=[jax.ShapeDtypeStruct((S, HEAD_W), F32)] * 2,
        compiler_params=_cparams("parallel"),
        name="rope_tables",
    )(positions.reshape(S, 1), inv)


def _rope(t, cos, sin, scale):
    lane = lax.broadcasted_iota(jnp.int32, t.shape, 1)
    first_half = (lane % HEAD_DIM) < (HEAD_DIM // 2)
    rot = jnp.where(first_half,
                    pltpu.roll(t, HEAD_W - HEAD_DIM // 2, 1),
                    pltpu.roll(t, HEAD_DIM // 2, 1))
    return (t * cos + rot * sin) * scale


_POINTWISE = {"ident": lambda t: t, "silu": _silu, "sigmoid": jax.nn.sigmoid, "gelu": _gelu}
_ROPE_SCALE = {"rope_q": Q_SCALE, "rope_k": 1.0}


def _inproj_body(x_ref, w_ref, cos_ref, sin_ref, o_ref, wb_ref, *, kind, rm, kchunk):
    tm, tn = o_ref.shape

    @pl.when(pl.program_id(1) == 0)
    def _():
        for k0 in range(0, w_ref.shape[0], kchunk):
            wb_ref[k0:k0 + kchunk, :] = w_ref[k0:k0 + kchunk, :].astype(wb_ref.dtype)

    for r0 in range(0, tm, rm):
        rows = slice(r0, r0 + rm)
        a = jnp.dot(x_ref[rows, :], wb_ref[...], preferred_element_type=F32)
        if kind in _ROPE_SCALE:
            cos, sin = cos_ref[rows, :], sin_ref[rows, :]
            for c0 in range(0, tn, HEAD_W):
                o_ref[rows, c0:c0 + HEAD_W] = _rope(a[:, c0:c0 + HEAD_W], cos, sin,
                                                    _ROPE_SCALE[kind]).astype(o_ref.dtype)
        else:
            o_ref[rows, :] = _POINTWISE[kind](a).astype(o_ref.dtype)


def _cast_body(x_ref, o_ref):
    o_ref[...] = x_ref[...].astype(o_ref.dtype)


def _to_bf16(a, *, block_bytes=8 * 1024 * 1024):
    L, R, C = a.shape
    tr = R
    while tr * C * 4 > block_bytes and tr % 16 == 0:
        tr //= 2
    return pl.pallas_call(
        _cast_body,
        grid=(L, R // tr),
        in_specs=[pl.BlockSpec((None, tr, C), lambda l, i: (l, i, 0))],
        out_specs=pl.BlockSpec((None, tr, C), lambda l, i: (l, i, 0)),
        out_shape=jax.ShapeDtypeStruct(a.shape, BF16),
        compiler_params=_cparams("parallel", "parallel"),
        name="to_bf16",
    )(a)


def _in_projection(xb, w, layer, cos, sin, *, kind, units, unit_w, tm=1024, tn=512, rm=128):
    S, D = xb.shape
    tm = min(tm, S)
    per_unit = unit_w // tn
    ncols = len(units) * unit_w

    def w_tile(j):
        first = jnp.int32(units[0] * per_unit)
        for n, unit in enumerate(units[1:], 1):
            first = jnp.where(j // per_unit == n, unit * per_unit, first)
        return first + j % per_unit

    return pl.pallas_call(
        functools.partial(_inproj_body, kind=kind, rm=rm, kchunk=512),
        grid=(ncols // tn, S // tm),
        in_specs=[pl.BlockSpec((tm, D), lambda j, i: (i, 0)),
                  pl.BlockSpec((None, D, tn), lambda j, i: (layer, 0, w_tile(j))),
                  pl.BlockSpec((tm, HEAD_W), lambda j, i: (i, 0)),
                  pl.BlockSpec((tm, HEAD_W), lambda j, i: (i, 0))],
        out_specs=pl.BlockSpec((tm, tn), lambda j, i: (i, j)),
        out_shape=jax.ShapeDtypeStruct((S, ncols), BF16),
        scratch_shapes=[pltpu.VMEM((D, tn), BF16)],
        compiler_params=_cparams("arbitrary", "arbitrary"),
        name="in_projection_" + kind,
    )(xb, w, cos, sin)


FIRST = 2
V_ROWS = HEAD_W + BF16_SUBLANES


def _attn_body(lam_ref, g_ref, q_ref, qn_ref, k_ref, v_ref, gate_ref, o_ref,
               qt_ref, vt_ref, st_ref, bm_ref, pt_ref, al_ref, m_ref, acc_ref,
               *, tq, tk, rc, lambda_init):
    qi = pl.program_id(1)
    halves = range(2)
    n_ch = tk // rc
    kc = 2 * rc
    nd = tq // tk
    assert tk % kc == 0 and tq % tk == 0 and nd % 2 == 0

    def by_vreg(a):
        return a.reshape(a.shape[0] // SUBLANES, SUBLANES, a.shape[1])

    all_q = slice(0, tq)

    def score(b, slot, qs=all_q):
        k = k_ref[pl.ds(pl.multiple_of(b * tk, tk), tk), :]
        bm = []
        for c in halves:
            st = jnp.dot(k, qt_ref[c, :, qs], preferred_element_type=F32)
            st_ref[slot, c, :, qs] = st
            bm.append(jnp.max(by_vreg(st), axis=0))
        return bm

    def score_end(slot, bm, qs=all_q):
        for c in halves:
            bm_ref[slot, c, :, qs] = jnp.max(bm[c], axis=0, keepdims=True)

    def load_scores(slot, c, r, masked, qs):
        st = st_ref[slot, c, r * rc:(r + 1) * rc, qs]
        if masked is not None:
            k_id = lax.broadcasted_iota(jnp.int32, st.shape, 0) + (r * rc + masked)
            q_id = lax.broadcasted_iota(jnp.int32, st.shape, 1) + qs.start
            st = jnp.where(k_id <= q_id, st, MASK_VALUE)
        return st

    def prob_begin(slot, masked, qs):
        if masked is not None:
            bmax = []
            for c in halves:
                bm = jnp.max(by_vreg(load_scores(slot, c, 0, masked, qs)), axis=0)
                for r in range(1, n_ch):
                    bm = jnp.maximum(bm, jnp.max(by_vreg(load_scores(slot, c, r, masked, qs)),
                                                 axis=0))
                bmax.append(jnp.max(bm, axis=0, keepdims=True))
        else:
            bmax = [bm_ref[slot, c, :, qs] for c in halves]
        m_prev = [m_ref[c, :, qs] for c in halves]
        m_cur = [jnp.maximum(m_prev[c], bmax[c]) for c in halves]
        alpha = [jnp.exp2(m_prev[c] - m_cur[c]) for c in halves]
        return m_cur, alpha

    def prob_chunk(slot, par, r, masked, qs, m_cur):
        for c in halves:
            x = load_scores(slot, c, r, masked, qs) - m_cur[c]
            pt_ref[par, c, r * rc:(r + 1) * rc, qs] = jnp.exp2(x.astype(pt_ref.dtype))

    def prob_end(par, qs, m_cur, alpha):
        for c in halves:
            al_ref[par, c, :, qs] = alpha[c]
            m_ref[c, :, qs] = m_cur[c]

    def value_chunk(b, par, j, qs, part):
        keys = slice(j * kc, (j + 1) * kc)
        vt = vt_ref[b, :, keys]
        for c in halves:
            pv = jnp.dot(vt, pt_ref[par, c, keys, qs], preferred_element_type=F32)
            part[c] = pv if part[c] is None else part[c] + pv

    def value_end(par, qs, part):
        for c in halves:
            acc_ref[c, :, qs] = al_ref[par, c, :, qs] * acc_ref[c, :, qs] + part[c]

    def consume(b, slot, par, masked, qs=all_q, next_slot=None, next_qs=all_q):
        m_cur, alpha = prob_begin(slot, masked, qs)
        part = [None, None]
        bm = score(b + 1, next_slot, next_qs) if next_slot is not None else None
        for r in range(n_ch):
            prob_chunk(slot, par, r, masked, qs, m_cur)
            if (r + 1) * rc % kc == 0:
                value_chunk(b, par, (r + 1) * rc // kc - 1, qs, part)
        if bm is not None:
            score_end(next_slot, bm, next_qs)
        prob_end(par, qs, m_cur, alpha)
        value_end(par, qs, part)

    def start_block(ref):
        qt = ref[...].astype(F32).T
        row = lax.broadcasted_iota(jnp.int32, qt.shape, 0)
        qt_ref[0] = jnp.where(row < HEAD_DIM, qt, 0.0).astype(qt_ref.dtype)
        qt_ref[1] = jnp.where(row >= HEAD_DIM, qt, 0.0).astype(qt_ref.dtype)
        score_end(FIRST, score(0, FIRST))

    @pl.when(qi == 0)
    def _():
        ones = jnp.ones((V_ROWS - HEAD_W, tk), vt_ref.dtype)
        for kb in range(vt_ref.shape[0]):
            vt_ref[kb, 0:HEAD_W, :] = (v_ref[kb * tk:(kb + 1) * tk, :].astype(F32).T
                                       .astype(vt_ref.dtype))
            vt_ref[kb, HEAD_W:V_ROWS, :] = ones
        start_block(q_ref)

    m_ref[...] = jnp.full(m_ref.shape, MASK_VALUE, F32)
    acc_ref[...] = jnp.zeros(acc_ref.shape, F32)

    def prob_only(slot, par):
        m_cur, alpha = prob_begin(slot, None, all_q)
        for r in range(n_ch):
            prob_chunk(slot, par, r, None, all_q, m_cur)
        prob_end(par, all_q, m_cur, alpha)

    def value_only(b, par):
        part = [None, None]
        for j in range(tk // kc):
            value_chunk(b, par, j, all_q, part)
        value_end(par, all_q, part)

    def trip3(b, slot, par, with_value=True):
        score_end(1 - par, score(b + 1, 1 - par))
        prob_only(slot, par)
        if with_value:
            value_only(b - 1, 1 - par)

    @pl.when(qi > 0)
    def _():
        trip3(0, FIRST, 0, with_value=False)

    def body(b, carry):
        lax.cond(b % 2 == 0, lambda: trip3(b, 0, 0), lambda: trip3(b, 1, 1))
        return carry

    lax.fori_loop(1, nd * qi, body, 0)

    def seen_by(d):
        return slice(d * tk, tq)

    def diagonal(d, slot, with_next):
        b, par = nd * qi + d, d % 2
        nxt = dict(next_slot=(d + 1) % 2, next_qs=seen_by(d + 1)) if with_next else {}
        consume(b, slot, par, d * tk, qs=slice(d * tk, (d + 1) * tk), **nxt)
        if d + 1 < nd:
            consume(b, slot, par, None, qs=seen_by(d + 1))

    for cond, slot in ((qi == 0, FIRST), (qi > 0, 0)):
        @pl.when(cond)
        def _():
            if slot != FIRST:
                value_only(nd * qi - 1, 1)
            diagonal(0, slot, True)

    for d in range(1, nd - 1):
        diagonal(d, d % 2, True)

    start_block(qn_ref)
    diagonal(nd - 1, (nd - 1) % 2, False)

    lv = lam_ref[...]
    lam = (jnp.exp(jnp.sum(lv[0:1] * lv[1:2], axis=1, keepdims=True))
           - jnp.exp(jnp.sum(lv[2:3] * lv[3:4], axis=1, keepdims=True)) + lambda_init)
    num = [acc_ref[c, 0:HEAD_W, :] for c in halves]
    den = [acc_ref[c, HEAD_W:HEAD_W + 1, :] for c in halves]
    ot = num[0] * (1.0 / den[0]) - lam * (num[1] * (1.0 / den[1]))
    ot = ot * lax.rsqrt(jnp.mean(ot * ot, axis=0, keepdims=True) + RMS_EPS)
    ot = ot * g_ref[...] * (1.0 - lambda_init)
    o_ref[...] = (ot.T * gate_ref[...].astype(F32)).astype(o_ref.dtype)


def _attention(q, k, v, gate, lam_vecs, subln_g, *, lambda_init, tq=1024, tk=512):
    S = q.shape[0]
    tq = min(tq, S)
    tk = min(tk, tq // 2)
    n_q = S // tq
    H = q.shape[1] // HEAD_W
    return pl.pallas_call(
        functools.partial(_attn_body, tq=tq, tk=tk, rc=min(128, tk // 2),
                          lambda_init=lambda_init),
        grid=(H, n_q),
        in_specs=[pl.BlockSpec((4, HEAD_DIM), lambda hh, i: (0, 0)),
                  pl.BlockSpec((HEAD_W, 1), lambda hh, i: (0, 0)),
                  pl.BlockSpec((tq, HEAD_W), lambda hh, i: (i, hh)),
                  pl.BlockSpec((tq, HEAD_W),
                               lambda hh, i: (jnp.minimum(i + 1, n_q - 1), hh)),
                  pl.BlockSpec((S, HEAD_W), lambda hh, i: (0, hh)),
                  pl.BlockSpec((S, HEAD_W), lambda hh, i: (0, hh)),
                  pl.BlockSpec((tq, HEAD_W), lambda hh, i: (i, hh))],
        out_specs=pl.BlockSpec((tq, HEAD_W), lambda hh, i: (i, hh)),
        out_shape=jax.ShapeDtypeStruct((S, H * HEAD_W), BF16),
        scratch_shapes=[pltpu.VMEM((2, HEAD_W, tq), BF16),
                        pltpu.VMEM((S // tk, V_ROWS, tk), BF16),
                        pltpu.VMEM((3, 2, tk, tq), F32),
                        pltpu.VMEM((3, 2, 1, tq), F32),
                        pltpu.VMEM((2, 2, tk, tq), BF16),
                        pltpu.VMEM((2, 2, 1, tq), F32),
                        pltpu.VMEM((2, 1, tq), F32),
                        pltpu.VMEM((2, V_ROWS, tq), F32)],
        compiler_params=_cparams("arbitrary", "arbitrary"),
        name="diff_attention",
    )(lam_vecs, subln_g.reshape(HEAD_W, 1), q, q, k, v, gate)


def _conv_body(val_ref, glu_ref, pval_ref, pglu_ref, gate_ref, dww_ref, dwb_ref, lng_ref,
               lnb_ref, pww_ref, pwb_ref, o_ref, ext_ref, sh_ref, y_ref, *, rc, cc):
    i = pl.program_id(0)
    T, C = y_ref.shape
    L = T + CONV_HALO
    prev = pval_ref[...].astype(F32) * pglu_ref[...].astype(F32)
    ext_ref[0:CONV_HALO, :] = jnp.where(i > 0, prev, 0.0)
    ext_ref[CONV_HALO:L, :] = val_ref[...].astype(F32) * glu_ref[...].astype(F32)
    ext_ref[L:L + SUBLANES, :] = jnp.zeros((SUBLANES, C), F32)

    for s in range(1, SUBLANES):
        for c0 in range(0, C, cc):
            sh_ref[s - 1, :, c0:c0 + cc] = ext_ref[s:s + L, c0:c0 + cc]

    def window(off, r0, c0):
        s, whole = off % SUBLANES, off - off % SUBLANES
        rows = slice(r0 + whole, r0 + whole + rc)
        return ext_ref[rows, c0:c0 + cc] if s == 0 else sh_ref[s - 1, rows, c0:c0 + cc]

    base = CONV_HALO - (CONV_K - 1)
    for r0 in range(0, T, rc):
        for c0 in range(0, C, cc):
            acc = jnp.zeros((rc, cc), F32) + dwb_ref[:, c0:c0 + cc]
            for j in range(CONV_K):
                acc = acc + dww_ref[j:j + 1, c0:c0 + cc] * window(base + j, r0, c0)
            y_ref[r0:r0 + rc, c0:c0 + cc] = acc

    a = _silu(_layer_norm_rows(y_ref[...], lng_ref[...], lnb_ref[...]))
    z = jnp.dot(a.astype(BF16), pww_ref[...], preferred_element_type=F32) + pwb_ref[...]
    o_ref[...] = (z * gate_ref[...].astype(F32)).astype(o_ref.dtype)


def _conv_branch(val, val_blk, glu, gate, gate_blk, dw_w, dw_b, ln_g, ln_b, pw_wb, layer, pw_b,
                 *, T=512):
    S, C = glu.shape
    T = min(T, S)
    hb = T // CONV_HALO
    row = lambda a: a.reshape(1, C)
    prev_rows = lambda i: jnp.maximum(i * hb - 1, 0)
    return pl.pallas_call(
        functools.partial(_conv_body, rc=64, cc=256),
        grid=(S // T,),
        in_specs=[pl.BlockSpec((T, C), lambda i: (i, val_blk)),
                  pl.BlockSpec((T, C), lambda i: (i, 0)),
                  pl.BlockSpec((CONV_HALO, C), lambda i: (prev_rows(i), val_blk)),
                  pl.BlockSpec((CONV_HALO, C), lambda i: (prev_rows(i), 0)),
                  pl.BlockSpec((T, C), lambda i: (i, gate_blk)),
                  pl.BlockSpec((CONV_K, C), lambda i: (0, 0)),
                  pl.BlockSpec((1, C), lambda i: (0, 0)),
                  pl.BlockSpec((1, C), lambda i: (0, 0)),
                  pl.BlockSpec((1, C), lambda i: (0, 0)),
                  pl.BlockSpec((None, C, C), lambda i: (layer, 0, 0)),
                  pl.BlockSpec((1, C), lambda i: (0, 0))],
        out_specs=pl.BlockSpec((T, C), lambda i: (i, 0)),
        out_shape=jax.ShapeDtypeStruct((S, C), BF16),
        scratch_shapes=[pltpu.VMEM((T + CONV_HALO + SUBLANES, C), F32),
                        pltpu.VMEM((SUBLANES - 1, T + CONV_HALO, C), F32),
                        pltpu.VMEM((T, C), F32)],
        compiler_params=_cparams("parallel"),
        name="conv_branch",
    )(val, glu, val, glu, gate, dw_w, row(dw_b), row(ln_g), row(ln_b), pw_wb, row(pw_b))


def _sgu_body(u_ref, v_ref, gate_ref, lng_ref, lnb_ref, w_ref, b_ref, o_ref, vn_ref):
    T, W = vn_ref.shape
    vn_ref[...] = _layer_norm_rows(v_ref[...].astype(F32), lng_ref[...], lnb_ref[...]).astype(BF16)
    r_id = lax.broadcasted_iota(jnp.int32, (CHUNK, CHUNK), 0)
    c_id = lax.broadcasted_iota(jnp.int32, (CHUNK, CHUNK), 1)
    causal = c_id <= r_id
    for g in range(W // SGU_GROUP_W):
        cols = slice(g * SGU_GROUP_W, (g + 1) * SGU_GROUP_W)
        wg = jnp.where(causal, w_ref[g], 0.0).astype(BF16)
        bias = b_ref[g]
        for n in range(T // CHUNK):
            rows = slice(n * CHUNK, (n + 1) * CHUNK)
            mixed = jnp.dot(wg, vn_ref[rows, cols], preferred_element_type=F32) + bias
            o_ref[rows, cols] = (u_ref[rows, cols].astype(F32) * mixed
                                 * gate_ref[rows, cols].astype(F32)).astype(o_ref.dtype)


def _sgu_branch(uv, gate, gate_blk, ln_g, ln_b, w_s, b_s, *, T=512):
    S = uv.shape[0]
    G = w_s.shape[0]
    W = G * SGU_GROUP_W
    T = min(T, S)
    return pl.pallas_call(
        _sgu_body,
        grid=(S // T,),
        in_specs=[pl.BlockSpec((T, W), lambda i: (i, 0)),
                  pl.BlockSpec((T, W), lambda i: (i, 1)),
                  pl.BlockSpec((T, W), lambda i: (i, gate_blk)),
                  pl.BlockSpec((1, W), lambda i: (0, 0)),
                  pl.BlockSpec((1, W), lambda i: (0, 0)),
                  pl.BlockSpec((G, CHUNK, CHUNK), lambda i: (0, 0, 0)),
                  pl.BlockSpec((G, CHUNK, 1), lambda i: (0, 0, 0))],
        out_specs=pl.BlockSpec((T, W), lambda i: (i, 0)),
        out_shape=jax.ShapeDtypeStruct((S, W), BF16),
        scratch_shapes=[pltpu.VMEM((T, W), BF16)],
        compiler_params=_cparams("parallel"),
        name="sgu_branch",
    )(uv, uv, gate, ln_g.reshape(1, W), ln_b.reshape(1, W), w_s, b_s.reshape(G, CHUNK, 1))


def _outproj_body(a_ref, c_ref, s_ref, wa_ref, wc_ref, ws_ref, x_ref, o_ref, *, alpha):
    y = jnp.dot(a_ref[...], wa_ref[...], preferred_element_type=F32)
    y = y + jnp.dot(c_ref[...], wc_ref[...], preferred_element_type=F32)
    y = y + jnp.dot(s_ref[...], ws_ref[...], preferred_element_type=F32)
    o_ref[...] = alpha * x_ref[...] + y


def _out_projection(attn, conv, sgu, w, layer, x, *, alpha, tm=1024, tn=1024):
    S, D = x.shape
    tm = min(tm, S)
    wa, wc, ws = attn.shape[1], conv.shape[1], sgu.shape[1]
    return pl.pallas_call(
        functools.partial(_outproj_body, alpha=alpha),
        grid=(S // tm, D // tn),
        in_specs=[pl.BlockSpec((tm, wa), lambda i, j: (i, 0)),
                  pl.BlockSpec((tm, wc), lambda i, j: (i, 0)),
                  pl.BlockSpec((tm, ws), lambda i, j: (i, 0)),
                  pl.BlockSpec((None, wa, tn), lambda i, j: (layer, 0, j)),
                  pl.BlockSpec((None, wc, tn), lambda i, j: (layer, wa // wc, j)),
                  pl.BlockSpec((None, ws, tn), lambda i, j: (layer, (wa + wc) // ws, j)),
                  pl.BlockSpec((tm, tn), lambda i, j: (i, j))],
        out_specs=pl.BlockSpec((tm, tn), lambda i, j: (i, j)),
        out_shape=jax.ShapeDtypeStruct((S, D), F32),
        compiler_params=_cparams("parallel", "arbitrary"),
        name="out_projection",
    )(attn, conv, sgu, w, w, w, x)


def _postln_body(r_ref, g_ref, b_ref, *o_refs):
    y = _layer_norm_rows(r_ref[...], g_ref[...], b_ref[...])
    for o_ref in o_refs:
        o_ref[...] = y.astype(o_ref.dtype)


def _post_layer_norm(r, g, b, *, with_bf16, tr=512):
    S, D = r.shape
    tr = min(tr, S)
    dtypes = (F32, BF16) if with_bf16 else (F32,)
    return pl.pallas_call(
        _postln_body,
        grid=(S // tr,),
        in_specs=[pl.BlockSpec((tr, D), lambda i: (i, 0)),
                  pl.BlockSpec((1, D), lambda i: (0, 0)),
                  pl.BlockSpec((1, D), lambda i: (0, 0))],
        out_specs=[pl.BlockSpec((tr, D), lambda i: (i, 0)) for _ in dtypes],
        out_shape=[jax.ShapeDtypeStruct((S, D), dt) for dt in dtypes],
        compiler_params=_cparams("parallel"),
        name="post_layer_norm",
    )(r, g.reshape(1, D), b.reshape(1, D))


def kernel(x, positions, w_in, attn_lambda, attn_subln_g, conv_dw_w, conv_dw_b, conv_ln_g, conv_ln_b, conv_pw_w, conv_pw_b, sgu_ln_g, sgu_ln_b, sgu_w, sgu_b, w_out, post_ln_g, post_ln_b):
    B, S, D = x.shape
    depth = w_in.shape[0]
    assert B == 1 and S % 1024 == 0 and D % 2048 == 0
    u = D // 4
    alpha = (2 * depth) ** 0.25
    groups = {"rope_q": (0, 1), "rope_k": (2, 3),
              "ident": (4, 5, 8),
              "silu": (6, 7, 10, 13),
              "sigmoid": (9,), "gelu": (11, 12)}

    cos, sin = _rope_tables(positions)
    xf = x.reshape(S, D)
    xb = _to_bf16(x)[0]
    w_out_b, pw_w_b = _to_bf16(w_out), _to_bf16(conv_pw_w)
    for layer in range(depth):
        lambda_init = 0.8 - 0.6 * math.exp(-0.3 * layer)
        h = {kind: _in_projection(xb, w_in, layer, cos, sin, kind=kind, units=units, unit_w=u)
             for kind, units in groups.items()}
        attn = _attention(h["rope_q"], h["rope_k"], h["ident"], h["silu"], attn_lambda[layer],
                          attn_subln_g[layer], lambda_init=lambda_init)
        conv = _conv_branch(h["ident"], 2, h["sigmoid"], h["silu"], 2, conv_dw_w[layer],
                            conv_dw_b[layer], conv_ln_g[layer], conv_ln_b[layer],
                            pw_w_b, layer, conv_pw_b[layer])
        sgu = _sgu_branch(h["gelu"], h["silu"], 3, sgu_ln_g[layer], sgu_ln_b[layer],
                          sgu_w[layer], sgu_b[layer])
        r = _out_projection(attn, conv, sgu, w_out_b, layer, xf, alpha=alpha)
        outs = _post_layer_norm(r, post_ln_g[layer], post_ln_b[layer],
                                with_bf16=layer + 1 < depth)
        xf, xb = outs[0], outs[-1]
    return xf.reshape(B, S, D)
```
